```python
import math
import jax, jax.numpy as jnp
from jax import lax
import numpy as np

D_MODEL = 1024
BATCH = 4
SEQ = 8192
DEPTH = 2

GRID_W = 64
Q_BLOCK = 128
ROPE_THETA = 10000.0
EPS = 1e-6

MLA_HEADS = 8
MLA_Q_LORA = 512
MLA_KV_LORA = 256
MLA_NOPE = 64
MLA_ROPE = 32
MLA_V = 64
GQA_Q_HEADS = 8
GQA_KV_HEADS = 2
GQA_HEAD_DIM = 64
DIFF_HEADS = 8
DIFF_QK = 64
DIFF_V = 2 * DIFF_QK
REL_BUCKETS = 32
REL_MAX_DIST = 128
MOE_GROUPS = 4
MOE_EXPERTS_PER_GROUP = 8
MOE_EXPERTS = MOE_GROUPS * MOE_EXPERTS_PER_GROUP
MOE_TOP_K = 2
MOE_HIDDEN = 512
MOE_BLOCK = 128

AB_SPLITS = [MLA_Q_LORA, MLA_KV_LORA, MLA_ROPE, GQA_Q_HEADS * GQA_HEAD_DIM, GQA_KV_HEADS * GQA_HEAD_DIM, GQA_KV_HEADS * GQA_HEAD_DIM]
AB_IN = sum(AB_SPLITS)
AB_OUT = MLA_HEADS * MLA_V + GQA_Q_HEADS * GQA_HEAD_DIM
C_IN = DIFF_HEADS * (4 * DIFF_QK + DIFF_V)
C_OUT = DIFF_HEADS * DIFF_V

kernel_name = "hybrid_mla_axialgqa_diffattn_hmoe_encoder"


def _rms(x, g):
    xf = x.astype(jnp.float32)
    y = xf * lax.rsqrt(jnp.mean(xf * xf, axis=-1, keepdims=True) + EPS)
    return (y * g.astype(jnp.float32)).astype(x.dtype)


def _rope_tables(pos, dim):
    inv = ROPE_THETA ** (-jnp.arange(0, dim, 2, dtype=jnp.float32) / dim)
    ang = pos.astype(jnp.float32)[:, None] * inv[None, :]
    ang = jnp.concatenate([ang, ang], axis=-1)
    return jnp.cos(ang), jnp.sin(ang)


def _apply_rope(x, cos, sin):
    half = x.shape[-1] // 2
    rot = jnp.concatenate([-x[..., half:], x[..., :half]], axis=-1)
    return x * cos[None, :, None, :].astype(x.dtype) + rot * sin[None, :, None, :].astype(x.dtype)


def _to_blocks(t):
    b, s = t.shape[:2]
    t = t.reshape((b, s // Q_BLOCK, Q_BLOCK) + t.shape[2:])
    return jnp.moveaxis(t, 1, 0)


def _from_blocks(t):
    t = jnp.moveaxis(t, 0, 1)
    return t.reshape((t.shape[0], t.shape[1] * t.shape[2]) + t.shape[3:])


def _block_softmax_attention(q, k, v):
    b, s, hq, d = q.shape
    hkv = k.shape[2]
    grp = hq // hkv
    dv = v.shape[-1]
    scale = 1.0 / math.sqrt(d)

    def one(qb):
        qg = qb.reshape(b, Q_BLOCK, hkv, grp, d)
        logits = jnp.einsum('bqhgd,bkhd->bhgqk', qg, k).astype(jnp.float32) * scale
        p = jax.nn.softmax(logits, axis=-1).astype(v.dtype)
        o = jnp.einsum('bhgqk,bkhe->bqhge', p, v)
        return o.reshape(b, Q_BLOCK, hq, dv)

    return _from_blocks(lax.map(one, _to_blocks(q)))


def _t5_bucket(rel):
    nb = REL_BUCKETS // 2
    max_exact = nb // 2
    ret = jnp.where(rel > 0, nb, 0)
    n = jnp.abs(rel)
    nf = jnp.maximum(n, 1).astype(jnp.float32)
    large = max_exact + (jnp.log(nf / max_exact) / math.log(REL_MAX_DIST / max_exact) * (nb - max_exact)).astype(jnp.int32)
    large = jnp.minimum(large, nb - 1)
    return ret + jnp.where(n < max_exact, n, large)


def _diff_attention(q, k, v, lam, rel_bias):
    b, s, h, _, d = q.shape
    scale = 1.0 / math.sqrt(d)
    kpos = jnp.arange(s, dtype=jnp.int32)

    def one(args):
        qb, start = args
        qpos = start + jnp.arange(Q_BLOCK, dtype=jnp.int32)
        bias = rel_bias[_t5_bucket(kpos[None, :] - qpos[:, None])]
        bias = jnp.transpose(bias, (2, 0, 1)).astype(jnp.float32)
        logits = jnp.einsum('bqhcd,bkhcd->bhcqk', qb, k).astype(jnp.float32) * scale + bias[None, :, None]
        p = jax.nn.softmax(logits, axis=-1)
        a = (p[:, :, 0] - lam * p[:, :, 1]).astype(v.dtype)
        return jnp.einsum('bhqk,bkhe->bqhe', a, v)

    starts = jnp.arange(s // Q_BLOCK, dtype=jnp.int32) * Q_BLOCK
    return _from_blocks(lax.map(one, (_to_blocks(q), starts)))


def _mixer_ab(h, w_in, q_a_norm, w_qb, kv_a_norm, w_kvb, mla_qn, mla_kn, gqa_qn, gqa_kn, w_out):
    b, s, _ = h.shape
    idx = [int(v) for v in np.cumsum(AB_SPLITS)[:-1]]
    cq, ckv, k_pe, qg, kg, vg = jnp.split(h @ w_in, idx, axis=-1)
    q = (_rms(cq, q_a_norm) @ w_qb).reshape(b, s, MLA_HEADS, MLA_NOPE + MLA_ROPE)
    kv = (_rms(ckv, kv_a_norm) @ w_kvb).reshape(b, s, MLA_HEADS, MLA_NOPE + MLA_V)
    k_nope, v_a = kv[..., :MLA_NOPE], kv[..., MLA_NOPE:]
    k_pe = jnp.broadcast_to(k_pe[:, :, None, :], (b, s, MLA_HEADS, MLA_ROPE))
    k = jnp.concatenate([k_nope, k_pe], axis=-1)
    q = _rms(q, mla_qn)
    k = _rms(k, mla_kn)
    cos, sin = _rope_tables(jnp.arange(s, dtype=jnp.int32), MLA_ROPE)
    q = jnp.concatenate([q[..., :MLA_NOPE], _apply_rope(q[..., MLA_NOPE:], cos, sin)], axis=-1)
    k = jnp.concatenate([k[..., :MLA_NOPE], _apply_rope(k[..., MLA_NOPE:], cos, sin)], axis=-1)
    o_a = _block_softmax_attention(q, k, v_a).reshape(b, s, MLA_HEADS * MLA_V)
    rows = s // GRID_W
    row = jnp.broadcast_to(jnp.arange(rows, dtype=jnp.int32)[:, None], (rows, GRID_W)).reshape(s)
    col = jnp.broadcast_to(jnp.arange(GRID_W, dtype=jnp.int32)[None, :], (rows, GRID_W)).reshape(s)
    half = GQA_HEAD_DIM // 2
    rc, rs = _rope_tables(row, half)
    cc, cs = _rope_tables(col, half)

    def axial(t):
        return jnp.concatenate([_apply_rope(t[..., :half], rc, rs), _apply_rope(t[..., half:], cc, cs)], axis=-1)

    qb = axial(_rms(qg.reshape(b, s, GQA_Q_HEADS, GQA_HEAD_DIM), gqa_qn))
    kb = axial(_rms(kg.reshape(b, s, GQA_KV_HEADS, GQA_HEAD_DIM), gqa_kn))
    vb = vg.reshape(b, s, GQA_KV_HEADS, GQA_HEAD_DIM)
    o_b = _block_softmax_attention(qb, kb, vb).reshape(b, s, GQA_Q_HEADS * GQA_HEAD_DIM)
    return jnp.concatenate([o_a, o_b], axis=-1) @ w_out


def _mixer_c(h, w_in, qn, kn, lam_q1, lam_k1, lam_q2, lam_k2, subln, w_out, rel_bias, lambda_init):
    b, s, _ = h.shape
    nq = DIFF_HEADS * 2 * DIFF_QK
    proj = h @ w_in
    q = _rms(proj[..., :nq].reshape(b, s, DIFF_HEADS, 2, DIFF_QK), qn)
    k = _rms(proj[..., nq:2 * nq].reshape(b, s, DIFF_HEADS, 2, DIFF_QK), kn)
    v = proj[..., 2 * nq:].reshape(b, s, DIFF_HEADS, DIFF_V)
    f32 = jnp.float32
    lam = (jnp.exp(jnp.sum(lam_q1.astype(f32) * lam_k1.astype(f32)))
           - jnp.exp(jnp.sum(lam_q2.astype(f32) * lam_k2.astype(f32))) + lambda_init)
    o = _diff_attention(q, k, v, lam, rel_bias)
    o = _rms(o, subln) * (1.0 - lambda_init)
    return o.reshape(b, s, C_OUT) @ w_out


def _hier_moe(h, w_group, b_group, w_expert, b_expert, w_up, w_down):
    b, s, d = h.shape
    t = b * s
    xt = h.reshape(t, d)
    pg = jax.nn.softmax((xt @ w_group).astype(jnp.float32) + b_group.astype(jnp.float32), axis=-1)
    pg_top, g_idx = lax.top_k(pg, 1)
    g = g_idx[:, 0]
    el = jnp.einsum('td,gde->tge', xt, w_expert).astype(jnp.float32) + b_expert.astype(jnp.float32)
    el = el[jnp.arange(t), g]
    pe = jax.nn.softmax(el, axis=-1)
    pe_top, e_top = lax.top_k(pe, MOE_TOP_K)
    gate = pg_top * pe_top / jnp.sum(pe_top, axis=-1, keepdims=True)
    expert = g[:, None] * MOE_EXPERTS_PER_GROUP + e_top
    n = t * MOE_TOP_K
    p_rows = n + MOE_EXPERTS * MOE_BLOCK
    n_blocks = p_rows // MOE_BLOCK
    flat_e = expert.reshape(n).astype(jnp.int32)
    flat_w = gate.reshape(n)
    flat_t = jnp.arange(n, dtype=jnp.int32) // MOE_TOP_K
    order = jnp.argsort(flat_e)
    se = flat_e[order]
    counts = jnp.zeros((MOE_EXPERTS,), jnp.int32).at[flat_e].add(1)
    starts = jnp.cumsum(counts) - counts
    padded = (counts + MOE_BLOCK - 1) // MOE_BLOCK * MOE_BLOCK
    pad_end = jnp.cumsum(padded)
    pad_start = pad_end - padded
    dest = pad_start[se] + (jnp.arange(n, dtype=jnp.int32) - starts[se])
    slot_t = jnp.zeros((p_rows,), jnp.int32).at[dest].set(flat_t[order])
    slot_w = jnp.zeros((p_rows,), jnp.float32).at[dest].set(flat_w[order])
    block_e = jnp.minimum(jnp.searchsorted(pad_end, jnp.arange(n_blocks, dtype=jnp.int32) * MOE_BLOCK, side='right'), MOE_EXPERTS - 1)
    xs = xt[slot_t].reshape(n_blocks, MOE_BLOCK, d)

    def expert_block(args):
        xb, e = args
        gu = xb @ w_up[e]
        return (jax.nn.silu(gu[:, :MOE_HIDDEN]) * gu[:, MOE_HIDDEN:]) @ w_down[e]

    ys = lax.map(expert_block, (xs, block_e)).reshape(p_rows, d)
    out = jax.ops.segment_sum(ys * slot_w[:, None].astype(ys.dtype), slot_t, num_segments=t)
    return out.reshape(b, s, d)


def setup_inputs(seed: int = 0) -> dict:
    key = jax.random.key(seed)
    keys = jax.random.split(key, 40)
    counter = [0]

    def nrm(shape, scale):
        k = keys[counter[0]]
        counter[0] += 1
        return jax.random.normal(k, shape, jnp.float32) * scale

    def gain(shape):
        return 1.0 + nrm(shape, 0.02)

    D = D_MODEL
    n_even = (DEPTH + 1) // 2
    n_odd = DEPTH // 2
    return {
        "x": nrm((BATCH, SEQ, D), 1.0),
        "c": nrm((BATCH, D), 1.0),
        "ada_w": nrm((DEPTH, D, 6 * D), 0.5 * D ** -0.5),
        "ada_b": nrm((DEPTH, 6 * D), 0.02),
        "norm_mix": gain((DEPTH, D)),
        "norm_ffn": gain((DEPTH, D)),
        "rel_bias": nrm((REL_BUCKETS, DIFF_HEADS), 0.5),
        "ab_w_in": nrm((n_even, D, AB_IN), D ** -0.5),
        "ab_q_a_norm": gain((n_even, MLA_Q_LORA)),
        "ab_w_qb": nrm((n_even, MLA_Q_LORA, MLA_HEADS * (MLA_NOPE + MLA_ROPE)), MLA_Q_LORA ** -0.5),
        "ab_kv_a_norm": gain((n_even, MLA_KV_LORA)),
        "ab_w_kvb": nrm((n_even, MLA_KV_LORA, MLA_HEADS * (MLA_NOPE + MLA_V)), MLA_KV_LORA ** -0.5),
        "ab_mla_qn": gain((n_even, MLA_NOPE + MLA_ROPE)),
        "ab_mla_kn": gain((n_even, MLA_NOPE + MLA_ROPE)),
        "ab_gqa_qn": gain((n_even, GQA_HEAD_DIM)),
        "ab_gqa_kn": gain((n_even, GQA_HEAD_DIM)),
        "ab_w_out": nrm((n_even, AB_OUT, D), AB_OUT ** -0.5),
        "c_w_in": nrm((n_odd, D, C_IN), D ** -0.5),
        "c_qn": gain((n_odd, 2, DIFF_QK)),
        "c_kn": gain((n_odd, 2, DIFF_QK)),
        "c_lam_q1": nrm((n_odd, DIFF_QK), 0.1),
        "c_lam_k1": nrm((n_odd, DIFF_QK), 0.1),
        "c_lam_q2": nrm((n_odd, DIFF_QK), 0.1),
        "c_lam_k2": nrm((n_odd, DIFF_QK), 0.1),
        "c_subln": gain((n_odd, DIFF_V)),
        "c_w_out": nrm((n_odd, C_OUT, D), C_OUT ** -0.5),
        "moe_w_group": nrm((DEPTH, D, MOE_GROUPS), D ** -0.5),
        "moe_b_group": nrm((DEPTH, MOE_GROUPS), 0.01),
        "moe_w_expert": nrm((DEPTH, MOE_GROUPS, D, MOE_EXPERTS_PER_GROUP), D ** -0.5),
        "moe_b_expert": nrm((DEPTH, MOE_GROUPS, MOE_EXPERTS_PER_GROUP), 0.01),
        "moe_w_up": nrm((DEPTH, MOE_EXPERTS, D, 2 * MOE_HIDDEN), D ** -0.5),
        "moe_w_down": nrm((DEPTH, MOE_EXPERTS, MOE_HIDDEN, D), MOE_HIDDEN ** -0.5),
    }


def reference(x, c, ada_w, ada_b, norm_mix, norm_ffn, rel_bias,
              ab_w_in, ab_q_a_norm, ab_w_qb, ab_kv_a_norm, ab_w_kvb, ab_mla_qn, ab_mla_kn,
              ab_gqa_qn, ab_gqa_kn, ab_w_out,
              c_w_in, c_qn, c_kn, c_lam_q1, c_lam_k1, c_lam_q2, c_lam_k2, c_subln, c_w_out,
              moe_w_group, moe_b_group, moe_w_expert, moe_b_expert, moe_w_up, moe_w_down):
    cond = jax.nn.silu(c)
    for l in range(DEPTH):
        mod = (cond @ ada_w[l] + ada_b[l])[:, None, :]
        shift1, scale1, gate1, shift2, scale2, gate2 = jnp.split(mod, 6, axis=-1)
        h = _rms(x, norm_mix[l]) * (1.0 + scale1) + shift1
        i = l // 2
        if l % 2 == 0:
            y = _mixer_ab(h, ab_w_in[i], ab_q_a_norm[i], ab_w_qb[i], ab_kv_a_norm[i], ab_w_kvb[i],
                          ab_mla_qn[i], ab_mla_kn[i], ab_gqa_qn[i], ab_gqa_kn[i], ab_w_out[i])
        else:
            lambda_init = 0.8 - 0.6 * math.exp(-0.3 * l)
            y = _mixer_c(h, c_w_in[i], c_qn[i], c_kn[i], c_lam_q1[i], c_lam_k1[i], c_lam_q2[i], c_lam_k2[i],
                         c_subln[i], c_w_out[i], rel_bias, lambda_init)
        x = x + gate1 * y
        h = _rms(x, norm_ffn[l]) * (1.0 + scale2) + shift2
        x = x + gate2 * _hier_moe(h, moe_w_group[l], moe_b_group[l], moe_w_expert[l], moe_b_expert[l],
                                   moe_w_up[l], moe_w_down[l])
    return x
```

```python
import functools
import math

import jax
import jax.numpy as jnp
from jax import lax
from jax.experimental import pallas as pl
from jax.experimental.pallas import tpu as pltpu

F32 = jnp.float32
BF16 = jnp.bfloat16
I32 = jnp.int32

EPS = 1e-6
ROPE_THETA = 10000.0
GRID_W = 64
LOG2E = 1.4426950408889634
NEG_BIG = -1e30

MLA_HEADS, MLA_Q_LORA, MLA_KV_LORA, MLA_NOPE, MLA_ROPE, MLA_V = 8, 512, 256, 64, 32, 64
GQA_Q_HEADS, GQA_KV_HEADS, GQA_HEAD_DIM = 8, 2, 64
DIFF_HEADS, DIFF_QK = 8, 64
DIFF_V = 2 * DIFF_QK
REL_BUCKETS, REL_MAX_DIST = 32, 128
MOE_GROUPS, MOE_EPG, MOE_TOP_K = 4, 8, 2
MOE_EXPERTS = MOE_GROUPS * MOE_EPG

HEAD_PAD = 128
ONES_ROWS = 16
VMEM_LIMIT = 48 * 1024 * 1024

TOK_TILE = 512
Q_TILE = 512
MOE_BLOCK = 256
COMBINE_TILE = 256

_NT = (((1,), (1,)), ((), ()))


def _cparams(sem):
    return pltpu.CompilerParams(dimension_semantics=sem, vmem_limit_bytes=VMEM_LIMIT)


def _full(shape):
    n = len(shape)
    return pl.BlockSpec(shape, lambda *_: (0,) * n)


def _rms_rows(xt, gcol):
    ms = jnp.mean(xt * xt, axis=0, keepdims=True)
    return xt * lax.rsqrt(ms + EPS) * gcol


def _rms_tok(x, grow):
    ms = jnp.mean(x * x, axis=-1, keepdims=True)
    return x * lax.rsqrt(ms + EPS) * grow


def _rope_rows(x, cos, sin_signed):
    half = x.shape[0] // 2
    sw = jnp.concatenate([x[half:], x[:half]], axis=0)
    return x * cos + sw * sin_signed


def _ones_block(tm):
    return (lax.broadcasted_iota(I32, (ONES_ROWS, tm), 0) == 0).astype(F32)


def _ada_kernel(c_ref, w_ref, b_ref, o_ref):
    c = c_ref[...]
    cond = c / (1.0 + jnp.exp(-c))
    o_ref[0] = jnp.dot(cond, w_ref[0], preferred_element_type=F32,
                       precision=lax.Precision.HIGHEST) + b_ref[0]


def _ada_mod(c, ada_w, ada_b):
    depth, d, n6 = ada_w.shape
    b = c.shape[0]
    rows = 8
    tn = 1536
    cp = jnp.zeros((rows, d), F32).at[:b].set(c)
    out = pl.pallas_call(
        _ada_kernel,
        out_shape=jax.ShapeDtypeStruct((depth, rows, n6), F32),
        grid=(depth, n6 // tn),
        in_specs=[_full((rows, d)),
                  pl.BlockSpec((1, d, tn), lambda l, j: (l, 0, j)),
                  pl.BlockSpec((1, 1, tn), lambda l, j: (l, 0, j))],
        out_specs=pl.BlockSpec((1, rows, tn), lambda l, j: (l, 0, j)),
        compiler_params=_cparams(("arbitrary", "arbitrary")),
        name="ada_mod",
    )(cp, ada_w, ada_b.reshape(depth, 1, n6))
    return out[:, :b].reshape(depth, b, 6, d)


def _front_ab_kernel(x_ref, mod_ref, gmix_ref, wint_ref, qan_ref, kvan_ref, wqbt_ref, wkvbt_ref,
                     mqn_ref, mkn_ref, gqn_ref, gkn_ref, cm_ref, sm_ref, cg_ref, sg_ref,
                     qa_ref, ka_ref, va_ref, qb_ref, kb_ref, vb_ref):
    tm = x_ref.shape[1]
    x = x_ref[0]
    shift, scale = mod_ref[0, 0:1, :], mod_ref[0, 1:2, :]
    h = _rms_tok(x, gmix_ref[...]) * (1.0 + scale) + shift
    proj = lax.dot_general(wint_ref[...], h.astype(BF16), _NT, preferred_element_type=F32)

    o_ckv = MLA_Q_LORA
    o_kpe = o_ckv + MLA_KV_LORA
    o_qg = o_kpe + MLA_ROPE
    o_kg = o_qg + GQA_Q_HEADS * GQA_HEAD_DIM
    o_vg = o_kg + GQA_KV_HEADS * GQA_HEAD_DIM
    cq = _rms_rows(proj[0:o_ckv], qan_ref[...]).astype(BF16)
    ckv = _rms_rows(proj[o_ckv:o_kpe], kvan_ref[...]).astype(BF16)
    kpe = proj[o_kpe:o_qg]
    q_all = jnp.dot(wqbt_ref[...], cq, preferred_element_type=F32)
    kv_all = jnp.dot(wkvbt_ref[...], ckv, preferred_element_type=F32)

    cm, sm = cm_ref[...], sm_ref[...]
    ones_blk = _ones_block(tm)
    dq = MLA_NOPE + MLA_ROPE
    zpad_a = jnp.zeros((HEAD_PAD - dq, tm), F32)
    qscale_a = LOG2E / math.sqrt(dq)
    mqn, mkn = mqn_ref[...], mkn_ref[...]
    kpe_ss = jnp.sum(kpe * kpe, axis=0, keepdims=True)
    for hd in range(MLA_HEADS):
        qh = _rms_rows(q_all[hd * dq:(hd + 1) * dq], mqn)
        qr = _rope_rows(qh[MLA_NOPE:], cm, sm)
        qfull = jnp.concatenate([qh[:MLA_NOPE], qr, zpad_a], axis=0) * qscale_a
        qa_ref[0, hd, 0] = qfull.astype(BF16)

        base = hd * (MLA_NOPE + MLA_V)
        kn = kv_all[base:base + MLA_NOPE]
        vh = kv_all[base + MLA_NOPE:base + MLA_NOPE + MLA_V]
        ms = (jnp.sum(kn * kn, axis=0, keepdims=True) + kpe_ss) * (1.0 / dq)
        r = lax.rsqrt(ms + EPS)
        kr = _rope_rows(kpe * r * mkn[MLA_NOPE:], cm, sm)
        kfull = jnp.concatenate([kn * r * mkn[:MLA_NOPE], kr, zpad_a], axis=0)
        ka_ref[0, hd] = kfull.T.astype(BF16)
        va_ref[0, hd, 0] = jnp.concatenate([vh, ones_blk], axis=0).astype(BF16)

    cg, sg = cg_ref[...], sg_ref[...]
    hh = GQA_HEAD_DIM // 2
    qscale_b = LOG2E / math.sqrt(GQA_HEAD_DIM)
    zpad_b = jnp.zeros((HEAD_PAD - GQA_HEAD_DIM, tm), F32)

    def axial(t):
        return jnp.concatenate([_rope_rows(t[:hh], cg[:hh], sg[:hh]),
                                _rope_rows(t[hh:], cg[hh:], sg[hh:])], axis=0)

    gqn, gkn = gqn_ref[...], gkn_ref[...]
    grp = GQA_Q_HEADS // GQA_KV_HEADS
    for hd in range(GQA_Q_HEADS):
        qh = proj[o_qg + hd * GQA_HEAD_DIM:o_qg + (hd + 1) * GQA_HEAD_DIM]
        qh = axial(_rms_rows(qh, gqn)) * qscale_b
        parts = [qh, zpad_b] if hd // grp == 0 else [zpad_b, qh]
        qb_ref[0, hd, 0] = jnp.concatenate(parts, axis=0).astype(BF16)
    kparts = []
    for g in range(GQA_KV_HEADS):
        kh = proj[o_kg + g * GQA_HEAD_DIM:o_kg + (g + 1) * GQA_HEAD_DIM]
        kparts.append(axial(_rms_rows(kh, gkn)))
        vh = proj[o_vg + g * GQA_HEAD_DIM:o_vg + (g + 1) * GQA_HEAD_DIM]
        vb_ref[0, g, 0] = jnp.concatenate([vh, ones_blk], axis=0).astype(BF16)
    kb_ref[0, 0] = jnp.concatenate(kparts, axis=0).T.astype(BF16)


def _col(v):
    return v.reshape(-1, 1).astype(F32)


def _rope_tables_t(pos, dim):
    inv = ROPE_THETA ** (-jnp.arange(0, dim, 2, dtype=F32) / dim)
    ang = pos.astype(F32)[:, None] * inv[None, :]
    ang = jnp.concatenate([ang, ang], axis=-1)
    sign = jnp.concatenate([-jnp.ones((dim // 2,), F32), jnp.ones((dim // 2,), F32)])
    return jnp.cos(ang).T, (jnp.sin(ang) * sign[None, :]).T


def _front_ab(x, mod, g_mix, w_in, q_a_norm, w_qb, kv_a_norm, w_kvb, mla_qn, mla_kn, gqa_qn, gqa_kn):
    b, s, d = x.shape
    tm = TOK_TILE
    nt = s // tm
    n_in = w_in.shape[1]
    pos = jnp.arange(s, dtype=I32)
    cm, sm = _rope_tables_t(pos, MLA_ROPE)
    half = GQA_HEAD_DIM // 2
    rc, rs = _rope_tables_t(pos // GRID_W, half)
    cc, cs = _rope_tables_t(pos % GRID_W, half)
    cg, sg = jnp.concatenate([rc, cc], axis=0), jnp.concatenate([rs, cs], axis=0)
    dve = MLA_V + ONES_ROWS
    tab = lambda rows: pl.BlockSpec((rows, tm), lambda bi, si: (0, si))
    outs = pl.pallas_call(
        _front_ab_kernel,
        out_shape=[
            jax.ShapeDtypeStruct((b, MLA_HEADS, 1, HEAD_PAD, s), BF16),
            jax.ShapeDtypeStruct((b, MLA_HEADS, s, HEAD_PAD), BF16),
            jax.ShapeDtypeStruct((b, MLA_HEADS, nt, dve, tm), BF16),
            jax.ShapeDtypeStruct((b, GQA_Q_HEADS, 1, HEAD_PAD, s), BF16),
            jax.ShapeDtypeStruct((b, 1, s, HEAD_PAD), BF16),
            jax.ShapeDtypeStruct((b, GQA_KV_HEADS, nt, dve, tm), BF16),
        ],
        grid=(b, nt),
        in_specs=[
            pl.BlockSpec((1, tm, d), lambda bi, si: (bi, si, 0)),
            pl.BlockSpec((1, 6, d), lambda bi, si: (bi, 0, 0)),
            _full((1, d)),
            _full((n_in, d)),
            _full((MLA_Q_LORA, 1)), _full((MLA_KV_LORA, 1)),
            _full((w_qb.shape[1], MLA_Q_LORA)), _full((w_kvb.shape[1], MLA_KV_LORA)),
            _full((MLA_NOPE + MLA_ROPE, 1)), _full((MLA_NOPE + MLA_ROPE, 1)),
            _full((GQA_HEAD_DIM, 1)), _full((GQA_HEAD_DIM, 1)),
            tab(MLA_ROPE), tab(MLA_ROPE), tab(GQA_HEAD_DIM), tab(GQA_HEAD_DIM),
        ],
        out_specs=[
            pl.BlockSpec((1, MLA_HEADS, 1, HEAD_PAD, tm), lambda bi, si: (bi, 0, 0, 0, si)),
            pl.BlockSpec((1, MLA_HEADS, tm, HEAD_PAD), lambda bi, si: (bi, 0, si, 0)),
            pl.BlockSpec((1, MLA_HEADS, 1, dve, tm), lambda bi, si: (bi, 0, si, 0, 0)),
            pl.BlockSpec((1, GQA_Q_HEADS, 1, HEAD_PAD, tm), lambda bi, si: (bi, 0, 0, 0, si)),
            pl.BlockSpec((1, 1, tm, HEAD_PAD), lambda bi, si: (bi, 0, si, 0)),
            pl.BlockSpec((1, GQA_KV_HEADS, 1, dve, tm), lambda bi, si: (bi, 0, si, 0, 0)),
        ],
        compiler_params=_cparams(("parallel", "parallel")),
        name="front_ab",
    )(x, mod, g_mix.reshape(1, d), w_in.T.astype(BF16), _col(q_a_norm), _col(kv_a_norm),
      w_qb.T.astype(BF16), w_kvb.T.astype(BF16), _col(mla_qn), _col(mla_kn), _col(gqa_qn), _col(gqa_kn),
      cm, sm, cg, sg)
    return outs


def _front_c_kernel(x_ref, mod_ref, gmix_ref, wint_ref, qn_ref, kn_ref, q_ref, k_ref, v_ref):
    tm = x_ref.shape[1]
    x = x_ref[0]
    shift, scale = mod_ref[0, 0:1, :], mod_ref[0, 1:2, :]
    h = _rms_tok(x, gmix_ref[...]) * (1.0 + scale) + shift
    proj = lax.dot_general(wint_ref[...], h.astype(BF16), _NT, preferred_element_type=F32)
    nq = DIFF_HEADS * 2 * DIFF_QK
    qn, kn = qn_ref[...], kn_ref[...]
    qscale = LOG2E / math.sqrt(DIFF_QK)
    zpad = jnp.zeros((DIFF_QK, tm), F32)
    ones_blk = _ones_block(tm)
    for hd in range(DIFF_HEADS):
        kparts = []
        for c in range(2):
            r0 = (hd * 2 + c) * DIFF_QK
            qc = _rms_rows(proj[r0:r0 + DIFF_QK], qn[c * DIFF_QK:(c + 1) * DIFF_QK]) * qscale
            parts = [qc, zpad] if c == 0 else [zpad, qc]
            q_ref[0, hd, c] = jnp.concatenate(parts, axis=0).astype(BF16)
            kparts.append(_rms_rows(proj[nq + r0:nq + r0 + DIFF_QK], kn[c * DIFF_QK:(c + 1) * DIFF_QK]))
        k_ref[0, hd] = jnp.concatenate(kparts, axis=0).T.astype(BF16)
        vh = proj[2 * nq + hd * DIFF_V:2 * nq + (hd + 1) * DIFF_V]
        v_ref[0, hd, 0] = jnp.concatenate([vh, ones_blk], axis=0).astype(BF16)


def _front_c(x, mod, g_mix, w_in, qn, kn):
    b, s, d = x.shape
    tm = TOK_TILE
    nt = s // tm
    dve = DIFF_V + ONES_ROWS
    return pl.pallas_call(
        _front_c_kernel,
        out_shape=[
            jax.ShapeDtypeStruct((b, DIFF_HEADS, 2, HEAD_PAD, s), BF16),
            jax.ShapeDtypeStruct((b, DIFF_HEADS, s, HEAD_PAD), BF16),
            jax.ShapeDtypeStruct((b, DIFF_HEADS, nt, dve, tm), BF16),
        ],
        grid=(b, nt),
        in_specs=[
            pl.BlockSpec((1, tm, d), lambda bi, si: (bi, si, 0)),
            pl.BlockSpec((1, 6, d), lambda bi, si: (bi, 0, 0)),
            _full((1, d)),
            _full((w_in.shape[1], d)),
            _full((2 * DIFF_QK, 1)), _full((2 * DIFF_QK, 1)),
        ],
        out_specs=[
            pl.BlockSpec((1, DIFF_HEADS, 2, HEAD_PAD, tm), lambda bi, si: (bi, 0, 0, 0, si)),
            pl.BlockSpec((1, DIFF_HEADS, tm, HEAD_PAD), lambda bi, si: (bi, 0, si, 0)),
            pl.BlockSpec((1, DIFF_HEADS, 1, dve, tm), lambda bi, si: (bi, 0, si, 0, 0)),
        ],
        compiler_params=_cparams(("parallel", "parallel")),
        name="front_c",
    )(x, mod, g_mix.reshape(1, d), w_in.T.astype(BF16), _col(qn), _col(kn))


def _attn_kernel(*refs, ncomp, dv, tk, nk, nbias, bias_step, bias_lo, diff_scale):
    has_bias = nbias > 0
    if has_bias:
        lam_ref, q_ref, k_ref, v_ref, bias_ref, subln_ref, o_ref, m_sc, acc_sc = refs
    else:
        q_ref, k_ref, v_ref, o_ref, m_sc, acc_sc = refs
    tq = q_ref.shape[-1]
    qi = pl.program_id(2)
    m_sc[...] = jnp.full(m_sc.shape, NEG_BIG, F32)
    acc_sc[...] = jnp.zeros(acc_sc.shape, F32)

    def step(j, carry):
        k_t = k_ref[0, 0, pl.ds(pl.multiple_of(j * tk, tk), tk), :]
        v_t = v_ref[0, 0, j]
        if has_bias:
            d = j * tk - qi * tq
            bidx = jnp.clip((d - bias_lo) // bias_step, 0, nbias - 1)
        for c in range(ncomp):
            s = jnp.dot(k_t, q_ref[0, 0, c], preferred_element_type=F32)
            if has_bias:
                s = s + bias_ref[0, bidx]
            m_prev = m_sc[c]
            m_new = jnp.maximum(m_prev, jnp.max(s, axis=0, keepdims=True))
            alpha = jnp.exp2(m_prev - m_new)
            p = jnp.exp2(s - m_new).astype(BF16)
            acc_sc[c] = alpha * acc_sc[c] + jnp.dot(v_t, p, preferred_element_type=F32)
            m_sc[c] = m_new
        return carry

    lax.fori_loop(0, nk, step, 0)

    if ncomp == 1:
        acc = acc_sc[0]
        o_ref[0] = (acc[:dv] / acc[dv:dv + 1]).astype(o_ref.dtype)
    else:
        a0, a1 = acc_sc[0], acc_sc[1]
        o = a0[:dv] / a0[dv:dv + 1] - lam_ref[0] * (a1[:dv] / a1[dv:dv + 1])
        o = _rms_rows(o, subln_ref[...]) * diff_scale
        o_ref[0] = o.astype(o_ref.dtype)


def _attention(q_t, k, v_t, *, dv, k_div, v_div, bias=None, bias_step=0, bias_lo=0, lam=None, subln=None,
               diff_scale=1.0):
    b, hq, ncomp, dpad, s = q_t.shape
    nk, dve, tk = v_t.shape[2:]
    tq = Q_TILE
    nq = s // tq
    nbias = 0 if bias is None else bias.shape[1]
    in_specs = [
        pl.BlockSpec((1, 1, ncomp, dpad, tq), lambda bi, hi, qi: (bi, hi, 0, 0, qi)),
        pl.BlockSpec((1, 1, s, dpad), lambda bi, hi, qi: (bi, hi // k_div, 0, 0)),
        pl.BlockSpec((1, 1, nk, dve, tk), lambda bi, hi, qi: (bi, hi // v_div, 0, 0, 0)),
    ]
    args = [q_t, k, v_t]
    if bias is not None:
        in_specs = [pl.BlockSpec(memory_space=pltpu.SMEM)] + in_specs + [
            pl.BlockSpec((1, nbias, tk, tq), lambda bi, hi, qi: (hi, 0, 0, 0)),
            _full((dv, 1)),
        ]
        args = [lam] + args + [bias, subln]
    kern = functools.partial(_attn_kernel, ncomp=ncomp, dv=dv, tk=tk, nk=nk, nbias=nbias,
                             bias_step=bias_step, bias_lo=bias_lo, diff_scale=diff_scale)
    return pl.pallas_call(
        kern,
        out_shape=jax.ShapeDtypeStruct((b, hq * dv, s), BF16),
        grid=(b, hq, nq),
        in_specs=in_specs,
        out_specs=pl.BlockSpec((1, dv, tq), lambda bi, hi, qi: (bi, hi, qi)),
        scratch_shapes=[pltpu.VMEM((ncomp, 1, tq), F32), pltpu.VMEM((ncomp, dve, tq), F32)],
        compiler_params=_cparams(("parallel", "parallel", "arbitrary")),
        name="attn_diff" if ncomp == 2 else "attn_softmax",
    )(*args)


def _t5_bucket(rel):
    nb = REL_BUCKETS // 2
    max_exact = nb // 2
    ret = jnp.where(rel > 0, nb, 0)
    n = jnp.abs(rel)
    nf = jnp.maximum(n, 1).astype(F32)
    large = max_exact + (jnp.log(nf / max_exact) / math.log(REL_MAX_DIST / max_exact) * (nb - max_exact)).astype(I32)
    large = jnp.minimum(large, nb - 1)
    return ret + jnp.where(n < max_exact, n, large)


def _bias_tiles(rel_bias, tk, tq):
    step = math.gcd(tk, tq)
    lo = -(-(REL_MAX_DIST - 1 + tk) // step) * step
    hi = -(-(REL_MAX_DIST - 1 + tq) // step) * step
    lo = -lo
    offs = jnp.arange(lo, hi + 1, step, dtype=I32)
    rel = offs[:, None, None] + jnp.arange(tk, dtype=I32)[None, :, None] - jnp.arange(tq, dtype=I32)[None, None, :]
    tiles = rel_bias[_t5_bucket(rel)]
    tiles = jnp.transpose(tiles, (3, 0, 1, 2)).astype(F32) * LOG2E
    return tiles, step, lo


def _post_kernel(*refs, n_o):
    o_refs = refs[:n_o]
    (woutt_ref, x_ref, mod_ref, gffn_ref, wrt_ref, br_ref,
     x1_ref, h2_ref, ri_ref, rf_ref, cnt_ref, carry_sc) = refs[n_o:]
    tm = x_ref.shape[1]
    first = jnp.logical_and(pl.program_id(0) == 0, pl.program_id(1) == 0)

    @pl.when(first)
    def _():
        carry_sc[...] = jnp.zeros(carry_sc.shape, F32)

    off = 0
    yt = None
    for o_ref in o_refs:
        n = o_ref.shape[1]
        part = jnp.dot(woutt_ref[:, off:off + n], o_ref[0], preferred_element_type=F32)
        yt = part if yt is None else yt + part
        off += n
    y = yt.T
    gate1 = mod_ref[0, 2:3, :]
    x1 = x_ref[0] + gate1 * y
    x1_ref[0] = x1
    shift2, scale2 = mod_ref[0, 3:4, :], mod_ref[0, 4:5, :]
    h2 = _rms_tok(x1, gffn_ref[...]) * (1.0 + scale2) + shift2
    h2_ref[0] = h2

    lg = lax.dot_general(wrt_ref[...], h2, _NT, preferred_element_type=F32,
                         precision=lax.Precision.HIGHEST) + br_ref[...]
    g_log = lg[0:MOE_GROUPS]
    gmax = jnp.max(g_log, axis=0, keepdims=True)
    iota_g = lax.broadcasted_iota(I32, (MOE_GROUPS, tm), 0)
    g_idx = jnp.min(jnp.where(g_log == gmax, iota_g, MOE_GROUPS), axis=0, keepdims=True)
    pg_top = 1.0 / jnp.sum(jnp.exp(g_log - gmax), axis=0, keepdims=True)
    el = jnp.zeros((MOE_EPG, tm), F32)
    for g in range(MOE_GROUPS):
        el = el + jnp.where(g_idx == g, lg[MOE_GROUPS + g * MOE_EPG:MOE_GROUPS + (g + 1) * MOE_EPG], 0.0)
    iota_e = lax.broadcasted_iota(I32, (MOE_EPG, tm), 0)
    e_max = jnp.max(el, axis=0, keepdims=True)
    i0 = jnp.min(jnp.where(el == e_max, iota_e, MOE_EPG), axis=0, keepdims=True)
    el2 = jnp.where(iota_e == i0, -jnp.inf, el)
    e_max2 = jnp.max(el2, axis=0, keepdims=True)
    i1 = jnp.min(jnp.where(el2 == e_max2, iota_e, MOE_EPG), axis=0, keepdims=True)
    p1 = jnp.exp(e_max2 - e_max)
    w0 = pg_top / (1.0 + p1)
    w1 = pg_top * p1 / (1.0 + p1)
    e0 = g_idx * MOE_EPG + i0
    e1 = g_idx * MOE_EPG + i1

    iota_x = lax.broadcasted_iota(I32, (MOE_EXPERTS, tm), 0)
    hit0, hit1 = iota_x == e0, iota_x == e1
    oh = jnp.logical_or(hit0, hit1).astype(BF16)
    upper = (lax.broadcasted_iota(I32, (tm, tm), 0) < lax.broadcasted_iota(I32, (tm, tm), 1)).astype(BF16)
    prefix = jnp.dot(oh, upper, preferred_element_type=F32) + carry_sc[:, 0:1]
    rank0 = jnp.sum(jnp.where(hit0, prefix, 0.0), axis=0, keepdims=True)
    rank1 = jnp.sum(jnp.where(hit1, prefix, 0.0), axis=0, keepdims=True)
    total = carry_sc[...] + jnp.sum(oh.astype(F32), axis=1, keepdims=True)
    carry_sc[...] = total
    cnt_ref[...] = total.astype(I32)

    zi = jnp.zeros((4, tm), I32)
    ri_ref[0] = jnp.concatenate([e0, e1, rank0.astype(I32), rank1.astype(I32), zi], axis=0)
    rf_ref[0] = jnp.concatenate([w0, w1, jnp.zeros((6, tm), F32)], axis=0)


def _post(o_list, w_out, x, mod, g_ffn, w_group, b_group, w_expert, b_expert):
    b, s, d = x.shape
    tm = TOK_TILE
    nt = s // tm
    n_r = MOE_GROUPS + MOE_EXPERTS
    n_rp = 40
    w_r = jnp.concatenate([w_group.T, jnp.transpose(w_expert, (0, 2, 1)).reshape(MOE_EXPERTS, d),
                           jnp.zeros((n_rp - n_r, d), F32)], axis=0).astype(F32)
    b_r = jnp.concatenate([b_group, b_expert.reshape(-1), jnp.zeros((n_rp - n_r,), F32)]).reshape(n_rp, 1)
    n_o = len(o_list)
    in_specs = [pl.BlockSpec((1, o.shape[1], tm), lambda bi, si: (bi, 0, si)) for o in o_list] + [
        _full((d, w_out.shape[0])),
        pl.BlockSpec((1, tm, d), lambda bi, si: (bi, si, 0)),
        pl.BlockSpec((1, 6, d), lambda bi, si: (bi, 0, 0)),
        _full((1, d)),
        _full((n_rp, d)), _full((n_rp, 1)),
    ]
    return pl.pallas_call(
        functools.partial(_post_kernel, n_o=n_o),
        out_shape=[
            jax.ShapeDtypeStruct((b, s, d), F32),
            jax.ShapeDtypeStruct((b, s, d), F32),
            jax.ShapeDtypeStruct((b, 8, s), I32),
            jax.ShapeDtypeStruct((b, 8, s), F32),
            jax.ShapeDtypeStruct((MOE_EXPERTS, 128), I32),
        ],
        grid=(b, nt),
        in_specs=in_specs,
        out_specs=[
            pl.BlockSpec((1, tm, d), lambda bi, si: (bi, si, 0)),
            pl.BlockSpec((1, tm, d), lambda bi, si: (bi, si, 0)),
            pl.BlockSpec((1, 8, tm), lambda bi, si: (bi, 0, si)),
            pl.BlockSpec((1, 8, tm), lambda bi, si: (bi, 0, si)),
            _full((MOE_EXPERTS, 128)),
        ],
        scratch_shapes=[pltpu.VMEM((MOE_EXPERTS, 128), F32)],
        compiler_params=_cparams(("arbitrary", "arbitrary")),
        name="post_router",
    )(*o_list, w_out.T.astype(BF16), x, mod, g_ffn.reshape(1, d), w_r, b_r)


def _row_copy(src_ref, row, dst_ref, slot, sem):
    return pltpu.make_async_copy(src_ref.at[pl.ds(row, 1)], dst_ref.at[pl.ds(slot, 1)], sem)


def _expert_kernel(be_ref, nu_ref, cur_ref, nxt_ref, h2_ref, wup_ref, wdn_ref, ys_ref, buf, sem):
    i = pl.program_id(0)
    bm = buf.shape[1]
    n_used = nu_ref[0]
    slot = i % 2

    def issue(idx_ref, sl):
        def body(r, carry):
            _row_copy(h2_ref, idx_ref[0, 0, r], buf.at[sl], r, sem.at[sl]).start()
            return carry
        lax.fori_loop(0, bm, body, 0, unroll=8)

    @pl.when(i == 0)
    def _():
        issue(cur_ref, 0)

    @pl.when(i + 1 < n_used)
    def _():
        issue(nxt_ref, 1 - slot)

    @pl.when(i < n_used)
    def _():
        pltpu.make_async_copy(h2_ref.at[pl.ds(0, bm)], buf.at[slot], sem.at[slot]).wait()
        f = wdn_ref.shape[1]
        xb = buf[slot].astype(BF16)
        gu = jnp.dot(xb, wup_ref[0], preferred_element_type=F32)
        g, u = gu[:, :f], gu[:, f:]
        a = (g / (1.0 + jnp.exp(-g))) * u
        ys_ref[...] = jnp.dot(a.astype(BF16), wdn_ref[0], preferred_element_type=F32)

    @pl.when(i >= n_used)
    def _():
        ys_ref[...] = jnp.zeros(ys_ref.shape, F32)


def _experts(h2, slot_t, block_e, n_used, w_up, w_down):
    t, d = h2.shape
    bm = MOE_BLOCK
    n_blocks = slot_t.shape[0] // bm
    e, _, f2 = w_up.shape
    f = f2 // 2
    idx = slot_t.reshape(n_blocks, 1, bm)
    grid_spec = pltpu.PrefetchScalarGridSpec(
        num_scalar_prefetch=2,
        grid=(n_blocks,),
        in_specs=[
            pl.BlockSpec((1, 1, bm), lambda i, be, nu: (i, 0, 0), memory_space=pltpu.SMEM),
            pl.BlockSpec((1, 1, bm), lambda i, be, nu: (jnp.minimum(i + 1, n_blocks - 1), 0, 0),
                         memory_space=pltpu.SMEM),
            pl.BlockSpec(memory_space=pl.ANY),
            pl.BlockSpec((1, d, f2), lambda i, be, nu: (be[i], 0, 0)),
            pl.BlockSpec((1, f, d), lambda i, be, nu: (be[i], 0, 0)),
        ],
        out_specs=pl.BlockSpec((bm, d), lambda i, be, nu: (i, 0)),
        scratch_shapes=[pltpu.VMEM((2, bm, d), F32), pltpu.SemaphoreType.DMA((2,))],
    )
    return pl.pallas_call(
        _expert_kernel,
        out_shape=jax.ShapeDtypeStruct((n_blocks * bm, d), F32),
        grid_spec=grid_spec,
        compiler_params=_cparams(("arbitrary",)),
        name="moe_experts",
    )(block_e, n_used, idx, idx, h2, w_up.astype(BF16), w_down.astype(BF16))


def _combine_kernel(dst_ref, x_ref, mod_ref, gw_ref, ys_ref, o_ref, buf, sem):
    tc = x_ref.shape[0]
    for k in range(MOE_TOP_K):
        def body(r, carry, k=k):
            _row_copy(ys_ref, dst_ref[0, k, r], buf.at[k], r, sem.at[k]).start()
            return carry
        lax.fori_loop(0, tc, body, 0, unroll=8)
    acc = None
    for k in range(MOE_TOP_K):
        pltpu.make_async_copy(ys_ref.at[pl.ds(0, tc)], buf.at[k], sem.at[k]).wait()
        term = buf[k] * gw_ref[:, k:k + 1]
        acc = term if acc is None else acc + term
    o_ref[...] = x_ref[...] + mod_ref[0, 5:6, :] * acc


def _combine(x1, mod, dest, gates, ys):
    t, d = x1.shape
    tc = COMBINE_TILE
    b = mod.shape[0]
    per_b = t // b // tc
    return pl.pallas_call(
        _combine_kernel,
        out_shape=jax.ShapeDtypeStruct((t, d), F32),
        grid=(t // tc,),
        in_specs=[
            pl.BlockSpec((1, MOE_TOP_K, tc), lambda i: (i, 0, 0), memory_space=pltpu.SMEM),
            pl.BlockSpec((tc, d), lambda i: (i, 0)),
            pl.BlockSpec((1, 6, d), lambda i: (i // per_b, 0, 0)),
            pl.BlockSpec((tc, MOE_TOP_K), lambda i: (i, 0)),
            pl.BlockSpec(memory_space=pl.ANY),
        ],
        out_specs=pl.BlockSpec((tc, d), lambda i: (i, 0)),
        scratch_shapes=[pltpu.VMEM((MOE_TOP_K, tc, d), F32), pltpu.SemaphoreType.DMA((MOE_TOP_K,))],
        compiler_params=_cparams(("arbitrary",)),
        name="moe_combine",
    )(dest, x1, mod, gates, ys)


def _moe(x1, h2, ri, rf, counts, mod, w_up, w_down):
    b, s, d = x1.shape
    t = b * s
    bm = MOE_BLOCK
    n_blocks = (t * MOE_TOP_K) // bm + MOE_EXPERTS
    cnt = counts[:, 0]
    padded = (cnt + bm - 1) // bm * bm
    pad_end = jnp.cumsum(padded)
    pad_start = pad_end - padded
    n_used = (pad_end[-1] // bm).astype(I32).reshape(1)
    block_e = jnp.minimum(jnp.searchsorted(pad_end, jnp.arange(n_blocks, dtype=I32) * bm, side='right'),
                          MOE_EXPERTS - 1).astype(I32)
    e = ri[:, 0:2, :]
    dest = pad_start[e] + ri[:, 2:4, :]
    tok = jnp.broadcast_to(jnp.arange(t, dtype=I32).reshape(b, 1, s), (b, MOE_TOP_K, s))
    slot_t = jnp.zeros((n_blocks * bm,), I32).at[dest.reshape(-1)].set(tok.reshape(-1))
    ys = _experts(h2.reshape(t, d), slot_t, block_e, n_used, w_up, w_down)
    tc = COMBINE_TILE
    dest_c = dest.reshape(b, MOE_TOP_K, s // tc, tc).transpose(0, 2, 1, 3).reshape(t // tc, MOE_TOP_K, tc)
    gates = rf[:, 0:2, :].transpose(0, 2, 1).reshape(t, MOE_TOP_K)
    out = _combine(x1.reshape(t, d), mod, dest_c, gates, ys)
    return out.reshape(b, s, d)


def kernel(x, c, ada_w, ada_b, norm_mix, norm_ffn, rel_bias, ab_w_in, ab_q_a_norm, ab_w_qb, ab_kv_a_norm, ab_w_kvb, ab_mla_qn, ab_mla_kn, ab_gqa_qn, ab_gqa_kn, ab_w_out, c_w_in, c_qn, c_kn, c_lam_q1, c_lam_k1, c_lam_q2, c_lam_k2, c_subln, c_w_out, moe_w_group, moe_b_group, moe_w_expert, moe_b_expert, moe_w_up, moe_w_down):
    depth = ada_w.shape[0]
    mod_all = _ada_mod(c, ada_w, ada_b)
    for l in range(depth):
        mod = mod_all[l]
        i = l // 2
        if l % 2 == 0:
            qa, ka, va, qb, kb, vb = _front_ab(x, mod, norm_mix[l], ab_w_in[i], ab_q_a_norm[i], ab_w_qb[i],
                                               ab_kv_a_norm[i], ab_w_kvb[i], ab_mla_qn[i], ab_mla_kn[i],
                                               ab_gqa_qn[i], ab_gqa_kn[i])
            o_a = _attention(qa, ka, va, dv=MLA_V, k_div=1, v_div=1)
            o_b = _attention(qb, kb, vb, dv=GQA_HEAD_DIM, k_div=GQA_Q_HEADS, v_div=GQA_Q_HEADS // GQA_KV_HEADS)
            o_list, w_out = [o_a, o_b], ab_w_out[i]
        else:
            lambda_init = 0.8 - 0.6 * math.exp(-0.3 * l)
            q, k, v = _front_c(x, mod, norm_mix[l], c_w_in[i], c_qn[i], c_kn[i])
            lam = (jnp.exp(jnp.sum(c_lam_q1[i].astype(F32) * c_lam_k1[i].astype(F32)))
                   - jnp.exp(jnp.sum(c_lam_q2[i].astype(F32) * c_lam_k2[i].astype(F32))) + lambda_init)
            tiles, step, lo = _bias_tiles(rel_bias, TOK_TILE, Q_TILE)
            o_c = _attention(q, k, v, dv=DIFF_V, k_div=1, v_div=1, bias=tiles, bias_step=step, bias_lo=lo,
                             lam=lam.reshape(1).astype(F32), subln=_col(c_subln[i]),
                             diff_scale=1.0 - lambda_init)
            o_list, w_out = [o_c], c_w_out[i]
        x1, h2, ri, rf, counts = _post(o_list, w_out, x, mod, norm_ffn[l], moe_w_group[l], moe_b_group[l],
                                       moe_w_expert[l], moe_b_expert[l])
        x = _moe(x1, h2, ri, rf, counts, mod, moe_w_up[l], moe_w_down[l])
    return x
```

```python
import functools
import math

import jax
import jax.numpy as jnp
from jax import lax
from jax.experimental import pallas as pl
from jax.experimental.pallas import tpu as pltpu

F32 = jnp.float32
BF16 = jnp.bfloat16
I32 = jnp.int32

EPS = 1e-6
ROPE_THETA = 10000.0
GRID_W = 64
LOG2E = 1.4426950408889634
NEG_BIG = -1e30

MLA_HEADS, MLA_Q_LORA, MLA_KV_LORA, MLA_NOPE, MLA_ROPE, MLA_V = 8, 512, 256, 64, 32, 64
GQA_Q_HEADS, GQA_KV_HEADS, GQA_HEAD_DIM = 8, 2, 64
DIFF_HEADS, DIFF_QK = 8, 64
DIFF_V = 2 * DIFF_QK
REL_BUCKETS, REL_MAX_DIST = 32, 128
MOE_GROUPS, MOE_EPG, MOE_TOP_K = 4, 8, 2
MOE_EXPERTS = MOE_GROUPS * MOE_EPG

HEAD_PAD = 128
ONES_ROWS = 16
VMEM_LIMIT = 48 * 1024 * 1024

TOK_TILE = 512
Q_TILE = 512
MOE_BLOCK = 256
COMBINE_TILE = 256

_NT = (((1,), (1,)), ((), ()))


def _cparams(sem):
    return pltpu.CompilerParams(dimension_semantics=sem, vmem_limit_bytes=VMEM_LIMIT)


def _full(shape):
    n = len(shape)
    return pl.BlockSpec(shape, lambda *_: (0,) * n)


def _rms_rows(xt, gcol):
    ms = jnp.mean(xt * xt, axis=0, keepdims=True)
    return xt * lax.rsqrt(ms + EPS) * gcol


def _rms_tok(x, grow):
    ms = jnp.mean(x * x, axis=-1, keepdims=True)
    return x * lax.rsqrt(ms + EPS) * grow


def _rope_rows(x, cos, sin_signed):
    half = x.shape[0] // 2
    sw = jnp.concatenate([x[half:], x[:half]], axis=0)
    return x * cos + sw * sin_signed


def _ones_block(tm):
    return (lax.broadcasted_iota(I32, (ONES_ROWS, tm), 0) == 0).astype(F32)


def _ada_kernel(c_ref, w_ref, b_ref, o_ref):
    c = c_ref[...]
    cond = c / (1.0 + jnp.exp(-c))
    o_ref[0] = jnp.dot(cond, w_ref[0], preferred_element_type=F32,
                       precision=lax.Precision.HIGHEST) + b_ref[0]


def _ada_mod(c, ada_w, ada_b):
    depth, d, n6 = ada_w.shape
    b = c.shape[0]
    rows = 8
    tn = 1536
    cp = jnp.zeros((rows, d), F32).at[:b].set(c)
    out = pl.pallas_call(
        _ada_kernel,
        out_shape=jax.ShapeDtypeStruct((depth, rows, n6), F32),
        grid=(depth, n6 // tn),
        in_specs=[_full((rows, d)),
                  pl.BlockSpec((1, d, tn), lambda l, j: (l, 0, j)),
                  pl.BlockSpec((1, 1, tn), lambda l, j: (l, 0, j))],
        out_specs=pl.BlockSpec((1, rows, tn), lambda l, j: (l, 0, j)),
        compiler_params=_cparams(("arbitrary", "arbitrary")),
        name="ada_mod",
    )(cp, ada_w, ada_b.reshape(depth, 1, n6))
    return out[:, :b].reshape(depth, b, 6, d)


def _front_ab_kernel(x_ref, mod_ref, gmix_ref, wint_ref, qan_ref, kvan_ref, wqbt_ref, wkvbt_ref,
                     mqn_ref, mkn_ref, gqn_ref, gkn_ref, cm_ref, sm_ref, cg_ref, sg_ref,
                     qa_ref, ka_ref, va_ref, qb_ref, kb_ref, vb_ref):
    tm = x_ref.shape[1]
    x = x_ref[0]
    shift, scale = mod_ref[0, 0:1, :], mod_ref[0, 1:2, :]
    h = _rms_tok(x, gmix_ref[...]) * (1.0 + scale) + shift
    proj = lax.dot_general(wint_ref[...], h.astype(BF16), _NT, preferred_element_type=F32)

    o_ckv = MLA_Q_LORA
    o_kpe = o_ckv + MLA_KV_LORA
    o_qg = o_kpe + MLA_ROPE
    o_kg = o_qg + GQA_Q_HEADS * GQA_HEAD_DIM
    o_vg = o_kg + GQA_KV_HEADS * GQA_HEAD_DIM
    cq = _rms_rows(proj[0:o_ckv], qan_ref[...]).astype(BF16)
    ckv = _rms_rows(proj[o_ckv:o_kpe], kvan_ref[...]).astype(BF16)
    kpe = proj[o_kpe:o_qg]
    q_all = jnp.dot(wqbt_ref[...], cq, preferred_element_type=F32)
    kv_all = jnp.dot(wkvbt_ref[...], ckv, preferred_element_type=F32)

    cm, sm = cm_ref[...], sm_ref[...]
    ones_blk = _ones_block(tm)
    dq = MLA_NOPE + MLA_ROPE
    zpad_a = jnp.zeros((HEAD_PAD - dq, tm), F32)
    qscale_a = LOG2E / math.sqrt(dq)
    mqn, mkn = mqn_ref[...], mkn_ref[...]
    kpe_ss = jnp.sum(kpe * kpe, axis=0, keepdims=True)
    for hd in range(MLA_HEADS):
        qh = _rms_rows(q_all[hd * dq:(hd + 1) * dq], mqn)
        qr = _rope_rows(qh[MLA_NOPE:], cm, sm)
        qfull = jnp.concatenate([qh[:MLA_NOPE], qr, zpad_a], axis=0) * qscale_a
        qa_ref[0, hd, 0] = qfull.astype(BF16)

        base = hd * (MLA_NOPE + MLA_V)
        kn = kv_all[base:base + MLA_NOPE]
        vh = kv_all[base + MLA_NOPE:base + MLA_NOPE + MLA_V]
        ms = (jnp.sum(kn * kn, axis=0, keepdims=True) + kpe_ss) * (1.0 / dq)
        r = lax.rsqrt(ms + EPS)
        kr = _rope_rows(kpe * r * mkn[MLA_NOPE:], cm, sm)
        kfull = jnp.concatenate([kn * r * mkn[:MLA_NOPE], kr, zpad_a], axis=0)
        ka_ref[0, hd] = kfull.T.astype(BF16)
        va_ref[0, hd, 0] = jnp.concatenate([vh, ones_blk], axis=0).astype(BF16)

    cg, sg = cg_ref[...], sg_ref[...]
    hh = GQA_HEAD_DIM // 2
    qscale_b = LOG2E / math.sqrt(GQA_HEAD_DIM)
    zpad_b = jnp.zeros((HEAD_PAD - GQA_HEAD_DIM, tm), F32)

    def axial(t):
        return jnp.concatenate([_rope_rows(t[:hh], cg[:hh], sg[:hh]),
                                _rope_rows(t[hh:], cg[hh:], sg[hh:])], axis=0)

    gqn, gkn = gqn_ref[...], gkn_ref[...]
    grp = GQA_Q_HEADS // GQA_KV_HEADS
    for hd in range(GQA_Q_HEADS):
        qh = proj[o_qg + hd * GQA_HEAD_DIM:o_qg + (hd + 1) * GQA_HEAD_DIM]
        qh = axial(_rms_rows(qh, gqn)) * qscale_b
        parts = [qh, zpad_b] if hd // grp == 0 else [zpad_b, qh]
        qb_ref[0, hd, 0] = jnp.concatenate(parts, axis=0).astype(BF16)
    kparts = []
    for g in range(GQA_KV_HEADS):
        kh = proj[o_kg + g * GQA_HEAD_DIM:o_kg + (g + 1) * GQA_HEAD_DIM]
        kparts.append(axial(_rms_rows(kh, gkn)))
        vh = proj[o_vg + g * GQA_HEAD_DIM:o_vg + (g + 1) * GQA_HEAD_DIM]
        vb_ref[0, g, 0] = jnp.concatenate([vh, ones_blk], axis=0).astype(BF16)
    kb_ref[0, 0] = jnp.concatenate(kparts, axis=0).T.astype(BF16)


def _col(v):
    return v.reshape(-1, 1).astype(F32)


def _rope_tables_t(pos, dim):
    inv = ROPE_THETA ** (-jnp.arange(0, dim, 2, dtype=F32) / dim)
    ang = pos.astype(F32)[:, None] * inv[None, :]
    ang = jnp.concatenate([ang, ang], axis=-1)
    sign = jnp.concatenate([-jnp.ones((dim // 2,), F32), jnp.ones((dim // 2,), F32)])
    return jnp.cos(ang).T, (jnp.sin(ang) * sign[None, :]).T


def _front_ab(x, mod, g_mix, w_in, q_a_norm, w_qb, kv_a_norm, w_kvb, mla_qn, mla_kn, gqa_qn, gqa_kn):
    b, s, d = x.shape
    tm = TOK_TILE
    nt = s // tm
    n_in = w_in.shape[1]
    pos = jnp.arange(s, dtype=I32)
    cm, sm = _rope_tables_t(pos, MLA_ROPE)
    half = GQA_HEAD_DIM // 2
    rc, rs = _rope_tables_t(pos // GRID_W, half)
    cc, cs = _rope_tables_t(pos % GRID_W, half)
    cg, sg = jnp.concatenate([rc, cc], axis=0), jnp.concatenate([rs, cs], axis=0)
    dve = MLA_V + ONES_ROWS
    tab = lambda rows: pl.BlockSpec((rows, tm), lambda bi, si: (0, si))
    outs = pl.pallas_call(
        _front_ab_kernel,
        out_shape=[
            jax.ShapeDtypeStruct((b, MLA_HEADS, 1, HEAD_PAD, s), BF16),
            jax.ShapeDtypeStruct((b, MLA_HEADS, s, HEAD_PAD), BF16),
            jax.ShapeDtypeStruct((b, MLA_HEADS, nt, dve, tm), BF16),
            jax.ShapeDtypeStruct((b, GQA_Q_HEADS, 1, HEAD_PAD, s), BF16),
            jax.ShapeDtypeStruct((b, 1, s, HEAD_PAD), BF16),
            jax.ShapeDtypeStruct((b, GQA_KV_HEADS, nt, dve, tm), BF16),
        ],
        grid=(b, nt),
        in_specs=[
            pl.BlockSpec((1, tm, d), lambda bi, si: (bi, si, 0)),
            pl.BlockSpec((1, 6, d), lambda bi, si: (bi, 0, 0)),
            _full((1, d)),
            _full((n_in, d)),
            _full((MLA_Q_LORA, 1)), _full((MLA_KV_LORA, 1)),
            _full((w_qb.shape[1], MLA_Q_LORA)), _full((w_kvb.shape[1], MLA_KV_LORA)),
            _full((MLA_NOPE + MLA_ROPE, 1)), _full((MLA_NOPE + MLA_ROPE, 1)),
            _full((GQA_HEAD_DIM, 1)), _full((GQA_HEAD_DIM, 1)),
            tab(MLA_ROPE), tab(MLA_ROPE), tab(GQA_HEAD_DIM), tab(GQA_HEAD_DIM),
        ],
        out_specs=[
            pl.BlockSpec((1, MLA_HEADS, 1, HEAD_PAD, tm), lambda bi, si: (bi, 0, 0, 0, si)),
            pl.BlockSpec((1, MLA_HEADS, tm, HEAD_PAD), lambda bi, si: (bi, 0, si, 0)),
            pl.BlockSpec((1, MLA_HEADS, 1, dve, tm), lambda bi, si: (bi, 0, si, 0, 0)),
            pl.BlockSpec((1, GQA_Q_HEADS, 1, HEAD_PAD, tm), lambda bi, si: (bi, 0, 0, 0, si)),
            pl.BlockSpec((1, 1, tm, HEAD_PAD), lambda bi, si: (bi, 0, si, 0)),
            pl.BlockSpec((1, GQA_KV_HEADS, 1, dve, tm), lambda bi, si: (bi, 0, si, 0, 0)),
        ],
        compiler_params=_cparams(("parallel", "parallel")),
        name="front_ab",
    )(x, mod, g_mix.reshape(1, d), w_in.T.astype(BF16), _col(q_a_norm), _col(kv_a_norm),
      w_qb.T.astype(BF16), w_kvb.T.astype(BF16), _col(mla_qn), _col(mla_kn), _col(gqa_qn), _col(gqa_kn),
      cm, sm, cg, sg)
    return outs


def _front_c_kernel(x_ref, mod_ref, gmix_ref, wint_ref, qn_ref, kn_ref, q_ref, k_ref, v_ref):
    tm = x_ref.shape[1]
    x = x_ref[0]
    shift, scale = mod_ref[0, 0:1, :], mod_ref[0, 1:2, :]
    h = _rms_tok(x, gmix_ref[...]) * (1.0 + scale) + shift
    proj = lax.dot_general(wint_ref[...], h.astype(BF16), _NT, preferred_element_type=F32)
    nq = DIFF_HEADS * 2 * DIFF_QK
    qn, kn = qn_ref[...], kn_ref[...]
    qscale = LOG2E / math.sqrt(DIFF_QK)
    zpad = jnp.zeros((DIFF_QK, tm), F32)
    ones_blk = _ones_block(tm)
    for hd in range(DIFF_HEADS):
        kparts = []
        for c in range(2):
            r0 = (hd * 2 + c) * DIFF_QK
            qc = _rms_rows(proj[r0:r0 + DIFF_QK], qn[c * DIFF_QK:(c + 1) * DIFF_QK]) * qscale
            parts = [qc, zpad] if c == 0 else [zpad, qc]
            q_ref[0, hd, c] = jnp.concatenate(parts, axis=0).astype(BF16)
            kparts.append(_rms_rows(proj[nq + r0:nq + r0 + DIFF_QK], kn[c * DIFF_QK:(c + 1) * DIFF_QK]))
        k_ref[0, hd] = jnp.concatenate(kparts, axis=0).T.astype(BF16)
        vh = proj[2 * nq + hd * DIFF_V:2 * nq + (hd + 1) * DIFF_V]
        v_ref[0, hd, 0] = jnp.concatenate([vh, ones_blk], axis=0).astype(BF16)


def _front_c(x, mod, g_mix, w_in, qn, kn):
    b, s, d = x.shape
    tm = TOK_TILE
    nt = s // tm
    dve = DIFF_V + ONES_ROWS
    return pl.pallas_call(
        _front_c_kernel,
        out_shape=[
            jax.ShapeDtypeStruct((b, DIFF_HEADS, 2, HEAD_PAD, s), BF16),
            jax.ShapeDtypeStruct((b, DIFF_HEADS, s, HEAD_PAD), BF16),
            jax.ShapeDtypeStruct((b, DIFF_HEADS, nt, dve, tm), BF16),
        ],
        grid=(b, nt),
        in_specs=[
            pl.BlockSpec((1, tm, d), lambda bi, si: (bi, si, 0)),
            pl.BlockSpec((1, 6, d), lambda bi, si: (bi, 0, 0)),
            _full((1, d)),
            _full((w_in.shape[1], d)),
            _full((2 * DIFF_QK, 1)), _full((2 * DIFF_QK, 1)),
        ],
        out_specs=[
            pl.BlockSpec((1, DIFF_HEADS, 2, HEAD_PAD, tm), lambda bi, si: (bi, 0, 0, 0, si)),
            pl.BlockSpec((1, DIFF_HEADS, tm, HEAD_PAD), lambda bi, si: (bi, 0, si, 0)),
            pl.BlockSpec((1, DIFF_HEADS, 1, dve, tm), lambda bi, si: (bi, 0, si, 0, 0)),
        ],
        compiler_params=_cparams(("parallel", "parallel")),
        name="front_c",
    )(x, mod, g_mix.reshape(1, d), w_in.T.astype(BF16), _col(qn), _col(kn))


def _attn_kernel(*refs, ncomp, dv, tk, nk, nbias, bias_step, bias_lo, diff_scale):
    has_bias = nbias > 0
    if has_bias:
        lam_ref, q_ref, k_ref, v_ref, bias_ref, subln_ref, o_ref, m_sc, acc_sc, s_a, s_b = refs
    else:
        q_ref, k_ref, v_ref, o_ref, m_sc, acc_sc, s_a, s_b = refs
    tq = q_ref.shape[-1]
    qi = pl.program_id(2)
    m_sc[...] = jnp.full(m_sc.shape, NEG_BIG, F32)
    acc_sc[...] = jnp.zeros(acc_sc.shape, F32)

    n_bodies = nk // 2 if ncomp == 1 else nk

    def item(i, slot):
        return (2 * i + slot, 0) if ncomp == 1 else (i, slot)

    def scores(jc, s_ref):
        j, c = jc
        k_t = k_ref[0, 0, pl.ds(pl.multiple_of(j * tk, tk), tk), :]
        s = jnp.dot(k_t, q_ref[0, 0, c], preferred_element_type=F32)
        if has_bias:
            d = j * tk - qi * tq
            bidx = jnp.clip((d - bias_lo) // bias_step, 0, nbias - 1)
            s = s + bias_ref[0, bidx]
        s_ref[...] = s

    def consume(s_ref, jc):
        j, c = jc
        s = s_ref[...]
        m_prev = m_sc[c]
        m_new = jnp.maximum(m_prev, jnp.max(s, axis=0, keepdims=True))
        alpha = jnp.exp2(m_prev - m_new)
        p = jnp.exp2(s - m_new).astype(BF16)
        acc_sc[c] = alpha * acc_sc[c] + jnp.dot(v_ref[0, 0, j], p, preferred_element_type=F32)
        m_sc[c] = m_new

    scores(item(0, 0), s_a)

    def body(i, carry):
        scores(item(i, 1), s_b)
        consume(s_a, item(i, 0))
        scores(item(i + 1, 0), s_a)
        consume(s_b, item(i, 1))
        return carry

    lax.fori_loop(0, n_bodies - 1, body, 0)
    last = n_bodies - 1
    scores(item(last, 1), s_b)
    consume(s_a, item(last, 0))
    consume(s_b, item(last, 1))

    if ncomp == 1:
        acc = acc_sc[0]
        o_ref[0] = (acc[:dv] / acc[dv:dv + 1]).astype(o_ref.dtype)
    else:
        a0, a1 = acc_sc[0], acc_sc[1]
        o = a0[:dv] / a0[dv:dv + 1] - lam_ref[0] * (a1[:dv] / a1[dv:dv + 1])
        o = _rms_rows(o, subln_ref[...]) * diff_scale
        o_ref[0] = o.astype(o_ref.dtype)


def _attention(q_t, k, v_t, *, dv, k_div, v_div, bias=None, bias_step=0, bias_lo=0, lam=None, subln=None,
               diff_scale=1.0):
    b, hq, ncomp, dpad, s = q_t.shape
    nk, dve, tk = v_t.shape[2:]
    tq = Q_TILE
    nq = s // tq
    nbias = 0 if bias is None else bias.shape[1]
    in_specs = [
        pl.BlockSpec((1, 1, ncomp, dpad, tq), lambda bi, hi, qi: (bi, hi, 0, 0, qi)),
        pl.BlockSpec((1, 1, s, dpad), lambda bi, hi, qi: (bi, hi // k_div, 0, 0)),
        pl.BlockSpec((1, 1, nk, dve, tk), lambda bi, hi, qi: (bi, hi // v_div, 0, 0, 0)),
    ]
    args = [q_t, k, v_t]
    if bias is not None:
        in_specs = [pl.BlockSpec(memory_space=pltpu.SMEM)] + in_specs + [
            pl.BlockSpec((1, nbias, tk, tq), lambda bi, hi, qi: (hi, 0, 0, 0)),
            _full((dv, 1)),
        ]
        args = [lam] + args + [bias, subln]
    kern = functools.partial(_attn_kernel, ncomp=ncomp, dv=dv, tk=tk, nk=nk, nbias=nbias,
                             bias_step=bias_step, bias_lo=bias_lo, diff_scale=diff_scale)
    return pl.pallas_call(
        kern,
        out_shape=jax.ShapeDtypeStruct((b, hq * dv, s), BF16),
        grid=(b, hq, nq),
        in_specs=in_specs,
        out_specs=pl.BlockSpec((1, dv, tq), lambda bi, hi, qi: (bi, hi, qi)),
        scratch_shapes=[pltpu.VMEM((ncomp, 1, tq), F32), pltpu.VMEM((ncomp, dve, tq), F32),
                        pltpu.VMEM((tk, tq), F32), pltpu.VMEM((tk, tq), F32)],
        compiler_params=_cparams(("parallel", "parallel", "arbitrary")),
        name="attn_diff" if ncomp == 2 else "attn_softmax",
    )(*args)


def _t5_bucket(rel):
    nb = REL_BUCKETS // 2
    max_exact = nb // 2
    ret = jnp.where(rel > 0, nb, 0)
    n = jnp.abs(rel)
    nf = jnp.maximum(n, 1).astype(F32)
    large = max_exact + (jnp.log(nf / max_exact) / math.log(REL_MAX_DIST / max_exact) * (nb - max_exact)).astype(I32)
    large = jnp.minimum(large, nb - 1)
    return ret + jnp.where(n < max_exact, n, large)


def _bias_tiles(rel_bias, tk, tq):
    step = math.gcd(tk, tq)
    lo = -(-(REL_MAX_DIST - 1 + tk) // step) * step
    hi = -(-(REL_MAX_DIST - 1 + tq) // step) * step
    lo = -lo
    offs = jnp.arange(lo, hi + 1, step, dtype=I32)
    rel = offs[:, None, None] + jnp.arange(tk, dtype=I32)[None, :, None] - jnp.arange(tq, dtype=I32)[None, None, :]
    bucket = _t5_bucket(rel)[None]
    rb = rel_bias.astype(F32) * LOG2E
    tiles = jnp.zeros((rel_bias.shape[1],) + rel.shape, F32)
    for bkt in range(REL_BUCKETS):
        tiles = jnp.where(bucket == bkt, rb[bkt][:, None, None, None], tiles)
    return tiles, step, lo


def _post_kernel(*refs, n_o):
    o_refs = refs[:n_o]
    (woutt_ref, x_ref, mod_ref, gffn_ref, wrt_ref, br_ref,
     x1_ref, h2_ref, ri_ref, rf_ref, cnt_ref, carry_sc) = refs[n_o:]
    tm = x_ref.shape[1]
    first = jnp.logical_and(pl.program_id(0) == 0, pl.program_id(1) == 0)

    @pl.when(first)
    def _():
        carry_sc[...] = jnp.zeros(carry_sc.shape, F32)

    off = 0
    yt = None
    for o_ref in o_refs:
        n = o_ref.shape[1]
        part = jnp.dot(woutt_ref[:, off:off + n], o_ref[0], preferred_element_type=F32)
        yt = part if yt is None else yt + part
        off += n
    y = yt.T
    gate1 = mod_ref[0, 2:3, :]
    x1 = x_ref[0] + gate1 * y
    x1_ref[0] = x1
    shift2, scale2 = mod_ref[0, 3:4, :], mod_ref[0, 4:5, :]
    h2 = _rms_tok(x1, gffn_ref[...]) * (1.0 + scale2) + shift2
    h2_ref[0] = h2

    lg = lax.dot_general(wrt_ref[...], h2, _NT, preferred_element_type=F32,
                         precision=lax.Precision.HIGHEST) + br_ref[...]
    g_log = lg[0:MOE_GROUPS]
    gmax = jnp.max(g_log, axis=0, keepdims=True)
    iota_g = lax.broadcasted_iota(I32, (MOE_GROUPS, tm), 0)
    g_idx = jnp.min(jnp.where(g_log == gmax, iota_g, MOE_GROUPS), axis=0, keepdims=True)
    pg_top = 1.0 / jnp.sum(jnp.exp(g_log - gmax), axis=0, keepdims=True)
    el = jnp.zeros((MOE_EPG, tm), F32)
    for g in range(MOE_GROUPS):
        el = el + jnp.where(g_idx == g, lg[MOE_GROUPS + g * MOE_EPG:MOE_GROUPS + (g + 1) * MOE_EPG], 0.0)
    iota_e = lax.broadcasted_iota(I32, (MOE_EPG, tm), 0)
    e_max = jnp.max(el, axis=0, keepdims=True)
    i0 = jnp.min(jnp.where(el == e_max, iota_e, MOE_EPG), axis=0, keepdims=True)
    el2 = jnp.where(iota_e == i0, -jnp.inf, el)
    e_max2 = jnp.max(el2, axis=0, keepdims=True)
    i1 = jnp.min(jnp.where(el2 == e_max2, iota_e, MOE_EPG), axis=0, keepdims=True)
    p1 = jnp.exp(e_max2 - e_max)
    w0 = pg_top / (1.0 + p1)
    w1 = pg_top * p1 / (1.0 + p1)
    e0 = g_idx * MOE_EPG + i0
    e1 = g_idx * MOE_EPG + i1

    iota_x = lax.broadcasted_iota(I32, (MOE_EXPERTS, tm), 0)
    hit0, hit1 = iota_x == e0, iota_x == e1
    oh = jnp.logical_or(hit0, hit1).astype(BF16)
    upper = (lax.broadcasted_iota(I32, (tm, tm), 0) < lax.broadcasted_iota(I32, (tm, tm), 1)).astype(BF16)
    prefix = jnp.dot(oh, upper, preferred_element_type=F32) + carry_sc[:, 0:1]
    rank0 = jnp.sum(jnp.where(hit0, prefix, 0.0), axis=0, keepdims=True)
    rank1 = jnp.sum(jnp.where(hit1, prefix, 0.0), axis=0, keepdims=True)
    total = carry_sc[...] + jnp.sum(oh.astype(F32), axis=1, keepdims=True)
    carry_sc[...] = total
    cnt_ref[...] = total.astype(I32)

    zi = jnp.zeros((4, tm), I32)
    ri_ref[0] = jnp.concatenate([e0, e1, rank0.astype(I32), rank1.astype(I32), zi], axis=0)
    rf_ref[0] = jnp.concatenate([w0, w1, jnp.zeros((6, tm), F32)], axis=0)


def _post(o_list, w_out, x, mod, g_ffn, w_group, b_group, w_expert, b_expert):
    b, s, d = x.shape
    tm = TOK_TILE
    nt = s // tm
    n_r = MOE_GROUPS + MOE_EXPERTS
    n_rp = 40
    w_r = jnp.concatenate([w_group.T, jnp.transpose(w_expert, (0, 2, 1)).reshape(MOE_EXPERTS, d),
                           jnp.zeros((n_rp - n_r, d), F32)], axis=0).astype(F32)
    b_r = jnp.concatenate([b_group, b_expert.reshape(-1), jnp.zeros((n_rp - n_r,), F32)]).reshape(n_rp, 1)
    n_o = len(o_list)
    in_specs = [pl.BlockSpec((1, o.shape[1], tm), lambda bi, si: (bi, 0, si)) for o in o_list] + [
        _full((d, w_out.shape[0])),
        pl.BlockSpec((1, tm, d), lambda bi, si: (bi, si, 0)),
        pl.BlockSpec((1, 6, d), lambda bi, si: (bi, 0, 0)),
        _full((1, d)),
        _full((n_rp, d)), _full((n_rp, 1)),
    ]
    return pl.pallas_call(
        functools.partial(_post_kernel, n_o=n_o),
        out_shape=[
            jax.ShapeDtypeStruct((b, s, d), F32),
            jax.ShapeDtypeStruct((b, s, d), F32),
            jax.ShapeDtypeStruct((b, 8, s), I32),
            jax.ShapeDtypeStruct((b, 8, s), F32),
            jax.ShapeDtypeStruct((MOE_EXPERTS, 128), I32),
        ],
        grid=(b, nt),
        in_specs=in_specs,
        out_specs=[
            pl.BlockSpec((1, tm, d), lambda bi, si: (bi, si, 0)),
            pl.BlockSpec((1, tm, d), lambda bi, si: (bi, si, 0)),
            pl.BlockSpec((1, 8, tm), lambda bi, si: (bi, 0, si)),
            pl.BlockSpec((1, 8, tm), lambda bi, si: (bi, 0, si)),
            _full((MOE_EXPERTS, 128)),
        ],
        scratch_shapes=[pltpu.VMEM((MOE_EXPERTS, 128), F32)],
        compiler_params=_cparams(("arbitrary", "arbitrary")),
        name="post_router",
    )(*o_list, w_out.T.astype(BF16), x, mod, g_ffn.reshape(1, d), w_r, b_r)


def _row_copy(src_ref, row, dst_ref, slot, sem):
    return pltpu.make_async_copy(src_ref.at[pl.ds(row, 1)], dst_ref.at[pl.ds(slot, 1)], sem)


def _expert_kernel(be_ref, nu_ref, cur_ref, nxt_ref, h2_ref, wup_ref, wdn_ref, ys_ref, buf, sem):
    i = pl.program_id(0)
    bm = buf.shape[1]
    n_used = nu_ref[0]
    slot = i % 2

    def issue(idx_ref, sl):
        def body(r, carry):
            _row_copy(h2_ref, idx_ref[0, 0, r], buf.at[sl], r, sem.at[sl]).start()
            return carry
        lax.fori_loop(0, bm, body, 0, unroll=8)

    @pl.when(i == 0)
    def _():
        issue(cur_ref, 0)

    @pl.when(i + 1 < n_used)
    def _():
        issue(nxt_ref, 1 - slot)

    @pl.when(i < n_used)
    def _():
        pltpu.make_async_copy(h2_ref.at[pl.ds(0, bm)], buf.at[slot], sem.at[slot]).wait()
        f = wdn_ref.shape[1]
        xb = buf[slot].astype(BF16)
        gu = jnp.dot(xb, wup_ref[0], preferred_element_type=F32)
        g, u = gu[:, :f], gu[:, f:]
        a = (g / (1.0 + jnp.exp(-g))) * u
        ys_ref[...] = jnp.dot(a.astype(BF16), wdn_ref[0], preferred_element_type=F32)

    @pl.when(i >= n_used)
    def _():
        ys_ref[...] = jnp.zeros(ys_ref.shape, F32)


def _experts(h2, slot_t, block_e, n_used, w_up, w_down):
    t, d = h2.shape
    bm = MOE_BLOCK
    n_blocks = slot_t.shape[0] // bm
    e, _, f2 = w_up.shape
    f = f2 // 2
    idx = slot_t.reshape(n_blocks, 1, bm)
    grid_spec = pltpu.PrefetchScalarGridSpec(
        num_scalar_prefetch=2,
        grid=(n_blocks,),
        in_specs=[
            pl.BlockSpec((1, 1, bm), lambda i, be, nu: (i, 0, 0), memory_space=pltpu.SMEM),
            pl.BlockSpec((1, 1, bm), lambda i, be, nu: (jnp.minimum(i + 1, n_blocks - 1), 0, 0),
                         memory_space=pltpu.SMEM),
            pl.BlockSpec(memory_space=pl.ANY),
            pl.BlockSpec((1, d, f2), lambda i, be, nu: (be[i], 0, 0)),
            pl.BlockSpec((1, f, d), lambda i, be, nu: (be[i], 0, 0)),
        ],
        out_specs=pl.BlockSpec((bm, d), lambda i, be, nu: (i, 0)),
        scratch_shapes=[pltpu.VMEM((2, bm, d), F32), pltpu.SemaphoreType.DMA((2,))],
    )
    return pl.pallas_call(
        _expert_kernel,
        out_shape=jax.ShapeDtypeStruct((n_blocks * bm, d), F32),
        grid_spec=grid_spec,
        compiler_params=_cparams(("arbitrary",)),
        name="moe_experts",
    )(block_e, n_used, idx, idx, h2, w_up.astype(BF16), w_down.astype(BF16))


def _combine_kernel(dst_ref, x_ref, mod_ref, gw_ref, ys_ref, o_ref, buf, sem):
    tc = x_ref.shape[0]
    for k in range(MOE_TOP_K):
        def body(r, carry, k=k):
            _row_copy(ys_ref, dst_ref[0, k, r], buf.at[k], r, sem.at[k]).start()
            return carry
        lax.fori_loop(0, tc, body, 0, unroll=8)
    acc = None
    for k in range(MOE_TOP_K):
        pltpu.make_async_copy(ys_ref.at[pl.ds(0, tc)], buf.at[k], sem.at[k]).wait()
        term = buf[k] * gw_ref[:, k:k + 1]
        acc = term if acc is None else acc + term
    o_ref[...] = x_ref[...] + mod_ref[0, 5:6, :] * acc


def _combine(x1, mod, dest, gates, ys):
    t, d = x1.shape
    tc = COMBINE_TILE
    b = mod.shape[0]
    per_b = t // b // tc
    return pl.pallas_call(
        _combine_kernel,
        out_shape=jax.ShapeDtypeStruct((t, d), F32),
        grid=(t // tc,),
        in_specs=[
            pl.BlockSpec((1, MOE_TOP_K, tc), lambda i: (i, 0, 0), memory_space=pltpu.SMEM),
            pl.BlockSpec((tc, d), lambda i: (i, 0)),
            pl.BlockSpec((1, 6, d), lambda i: (i // per_b, 0, 0)),
            pl.BlockSpec((tc, MOE_TOP_K), lambda i: (i, 0)),
            pl.BlockSpec(memory_space=pl.ANY),
        ],
        out_specs=pl.BlockSpec((tc, d), lambda i: (i, 0)),
        scratch_shapes=[pltpu.VMEM((MOE_TOP_K, tc, d), F32), pltpu.SemaphoreType.DMA((MOE_TOP_K,))],
        compiler_params=_cparams(("arbitrary",)),
        name="moe_combine",
    )(dest, x1, mod, gates, ys)


def _moe(x1, h2, ri, rf, counts, mod, w_up, w_down):
    b, s, d = x1.shape
    t = b * s
    bm = MOE_BLOCK
    n_blocks = (t * MOE_TOP_K) // bm + MOE_EXPERTS
    cnt = counts[:, 0]
    padded = (cnt + bm - 1) // bm * bm
    pad_end = jnp.cumsum(padded)
    pad_start = pad_end - padded
    n_used = (pad_end[-1] // bm).astype(I32).reshape(1)
    blk0 = jnp.arange(n_blocks, dtype=I32) * bm
    block_e = jnp.minimum(jnp.sum((pad_end[None, :] <= blk0[:, None]).astype(I32), axis=1), MOE_EXPERTS - 1)
    e = ri[:, 0:2, :]
    dest = ri[:, 2:4, :]
    for ex in range(MOE_EXPERTS):
        dest = dest + jnp.where(e == ex, pad_start[ex], 0)
    tok = jnp.broadcast_to(jnp.arange(t, dtype=I32).reshape(b, 1, s), (b, MOE_TOP_K, s))
    slot_t = jnp.zeros((n_blocks * bm,), I32).at[dest.reshape(-1)].set(tok.reshape(-1))
    ys = _experts(h2.reshape(t, d), slot_t, block_e, n_used, w_up, w_down)
    tc = COMBINE_TILE
    dest_c = dest.reshape(b, MOE_TOP_K, s // tc, tc).transpose(0, 2, 1, 3).reshape(t // tc, MOE_TOP_K, tc)
    gates = rf[:, 0:2, :].transpose(0, 2, 1).reshape(t, MOE_TOP_K)
    out = _combine(x1.reshape(t, d), mod, dest_c, gates, ys)
    return out.reshape(b, s, d)


def kernel(x, c, ada_w, ada_b, norm_mix, norm_ffn, rel_bias, ab_w_in, ab_q_a_norm, ab_w_qb, ab_kv_a_norm, ab_w_kvb, ab_mla_qn, ab_mla_kn, ab_gqa_qn, ab_gqa_kn, ab_w_out, c_w_in, c_qn, c_kn, c_lam_q1, c_lam_k1, c_lam_q2, c_lam_k2, c_subln, c_w_out, moe_w_group, moe_b_group, moe_w_expert, moe_b_expert, moe_w_up, moe_w_down):
    depth = ada_w.shape[0]
    mod_all = _ada_mod(c, ada_w, ada_b)
    for l in range(depth):
        mod = mod_all[l]
        i = l // 2
        if l % 2 == 0:
            qa, ka, va, qb, kb, vb = _front_ab(x, mod, norm_mix[l], ab_w_in[i], ab_q_a_norm[i], ab_w_qb[i],
                                               ab_kv_a_norm[i], ab_w_kvb[i], ab_mla_qn[i], ab_mla_kn[i],
                                               ab_gqa_qn[i], ab_gqa_kn[i])
            o_a = _attention(qa, ka, va, dv=MLA_V, k_div=1, v_div=1)
            o_b = _attention(qb, kb, vb, dv=GQA_HEAD_DIM, k_div=GQA_Q_HEADS, v_div=GQA_Q_HEADS // GQA_KV_HEADS)
            o_list, w_out = [o_a, o_b], ab_w_out[i]
        else:
            lambda_init = 0.8 - 0.6 * math.exp(-0.3 * l)
            q, k, v = _front_c(x, mod, norm_mix[l], c_w_in[i], c_qn[i], c_kn[i])
            lam = (jnp.exp(jnp.sum(c_lam_q1[i].astype(F32) * c_lam_k1[i].astype(F32)))
                   - jnp.exp(jnp.sum(c_lam_q2[i].astype(F32) * c_lam_k2[i].astype(F32))) + lambda_init)
            tiles, step, lo = _bias_tiles(rel_bias, TOK_TILE, Q_TILE)
            o_c = _attention(q, k, v, dv=DIFF_V, k_div=1, v_div=1, bias=tiles, bias_step=step, bias_lo=lo,
                             lam=lam.reshape(1).astype(F32), subln=_col(c_subln[i]),
                             diff_scale=1.0 - lambda_init)
            o_list, w_out = [o_c], c_w_out[i]
        x1, h2, ri, rf, counts = _post(o_list, w_out, x, mod, norm_ffn[l], moe_w_group[l], moe_b_group[l],
                                       moe_w_expert[l], moe_b_expert[l])
        x = _moe(x1, h2, ri, rf, counts, mod, moe_w_up[l], moe_w_down[l])
    return x
```

```python
import functools
import math

import jax
import jax.numpy as jnp
from jax import lax
from jax.experimental import pallas as pl
from jax.experimental.pallas import tpu as pltpu

F32 = jnp.float32
BF16 = jnp.bfloat16
I32 = jnp.int32

EPS = 1e-6
ROPE_THETA = 10000.0
GRID_W = 64
LOG2E = 1.4426950408889634
NEG_BIG = -1e30

MLA_HEADS, MLA_Q_LORA, MLA_KV_LORA, MLA_NOPE, MLA_ROPE, MLA_V = 8, 512, 256, 64, 32, 64
GQA_Q_HEADS, GQA_KV_HEADS, GQA_HEAD_DIM = 8, 2, 64
DIFF_HEADS, DIFF_QK = 8, 64
DIFF_V = 2 * DIFF_QK
REL_BUCKETS, REL_MAX_DIST = 32, 128
MOE_GROUPS, MOE_EPG, MOE_TOP_K = 4, 8, 2
MOE_EXPERTS = MOE_GROUPS * MOE_EPG

HEAD_PAD = 128
ONES_ROWS = 16
VMEM_LIMIT = 48 * 1024 * 1024
ATTN_VMEM_LIMIT = 56 * 1024 * 1024
ATTN_ITEMS = 4

TOK_TILE = 512
MOE_BLOCK = 256
COMBINE_TILE = 256

_NT = (((1,), (1,)), ((), ()))


def _cparams(sem):
    return pltpu.CompilerParams(dimension_semantics=sem, vmem_limit_bytes=VMEM_LIMIT)


def _full(shape):
    n = len(shape)
    return pl.BlockSpec(shape, lambda *_: (0,) * n)


def _rms_rows(xt, gcol):
    ms = jnp.mean(xt * xt, axis=0, keepdims=True)
    return xt * lax.rsqrt(ms + EPS) * gcol


def _rms_tok(x, grow):
    ms = jnp.mean(x * x, axis=-1, keepdims=True)
    return x * lax.rsqrt(ms + EPS) * grow


def _rope_rows(x, cos, sin_signed):
    half = x.shape[0] // 2
    sw = jnp.concatenate([x[half:], x[:half]], axis=0)
    return x * cos + sw * sin_signed


def _ones_block(tm):
    return (lax.broadcasted_iota(I32, (ONES_ROWS, tm), 0) == 0).astype(F32)


def _ada_kernel(c_ref, w_ref, b_ref, o_ref):
    c = c_ref[...]
    cond = c / (1.0 + jnp.exp(-c))
    o_ref[0] = jnp.dot(cond, w_ref[0], preferred_element_type=F32,
                       precision=lax.Precision.HIGHEST) + b_ref[0]


def _ada_mod(c, ada_w, ada_b):
    depth, d, n6 = ada_w.shape
    b = c.shape[0]
    rows = 8
    tn = 1536
    cp = jnp.zeros((rows, d), F32).at[:b].set(c)
    out = pl.pallas_call(
        _ada_kernel,
        out_shape=jax.ShapeDtypeStruct((depth, rows, n6), F32),
        grid=(depth, n6 // tn),
        in_specs=[_full((rows, d)),
                  pl.BlockSpec((1, d, tn), lambda l, j: (l, 0, j)),
                  pl.BlockSpec((1, 1, tn), lambda l, j: (l, 0, j))],
        out_specs=pl.BlockSpec((1, rows, tn), lambda l, j: (l, 0, j)),
        compiler_params=_cparams(("arbitrary", "arbitrary")),
        name="ada_mod",
    )(cp, ada_w, ada_b.reshape(depth, 1, n6))
    return out[:, :b].reshape(depth, b, 6, d)


def _front_ab_kernel(x_ref, mod_ref, gmix_ref, wint_ref, qan_ref, kvan_ref, wqbt_ref, wkvbt_ref,
                     mqn_ref, mkn_ref, gqn_ref, gkn_ref, cm_ref, sm_ref, cg_ref, sg_ref,
                     qa_ref, ka_ref, va_ref, qb_ref, kb_ref, vb_ref):
    tm = x_ref.shape[1]
    x = x_ref[0]
    shift, scale = mod_ref[0, 0:1, :], mod_ref[0, 1:2, :]
    h = _rms_tok(x, gmix_ref[...]) * (1.0 + scale) + shift
    proj = lax.dot_general(wint_ref[...], h.astype(BF16), _NT, preferred_element_type=F32)

    o_ckv = MLA_Q_LORA
    o_kpe = o_ckv + MLA_KV_LORA
    o_qg = o_kpe + MLA_ROPE
    o_kg = o_qg + GQA_Q_HEADS * GQA_HEAD_DIM
    o_vg = o_kg + GQA_KV_HEADS * GQA_HEAD_DIM
    cq = _rms_rows(proj[0:o_ckv], qan_ref[...]).astype(BF16)
    ckv = _rms_rows(proj[o_ckv:o_kpe], kvan_ref[...]).astype(BF16)
    kpe = proj[o_kpe:o_qg]
    q_all = jnp.dot(wqbt_ref[...], cq, preferred_element_type=F32)
    kv_all = jnp.dot(wkvbt_ref[...], ckv, preferred_element_type=F32)

    cm, sm = cm_ref[...], sm_ref[...]
    ones_blk = _ones_block(tm)
    dq = MLA_NOPE + MLA_ROPE
    zpad_a = jnp.zeros((HEAD_PAD - dq, tm), F32)
    qscale_a = LOG2E / math.sqrt(dq)
    mqn, mkn = mqn_ref[...], mkn_ref[...]
    kpe_ss = jnp.sum(kpe * kpe, axis=0, keepdims=True)
    for hd in range(MLA_HEADS):
        qh = _rms_rows(q_all[hd * dq:(hd + 1) * dq], mqn)
        qr = _rope_rows(qh[MLA_NOPE:], cm, sm)
        qfull = jnp.concatenate([qh[:MLA_NOPE], qr, zpad_a], axis=0) * qscale_a
        qa_ref[0, hd, 0, 0] = qfull.astype(BF16)

        base = hd * (MLA_NOPE + MLA_V)
        kn = kv_all[base:base + MLA_NOPE]
        vh = kv_all[base + MLA_NOPE:base + MLA_NOPE + MLA_V]
        ms = (jnp.sum(kn * kn, axis=0, keepdims=True) + kpe_ss) * (1.0 / dq)
        r = lax.rsqrt(ms + EPS)
        kr = _rope_rows(kpe * r * mkn[MLA_NOPE:], cm, sm)
        kfull = jnp.concatenate([kn * r * mkn[:MLA_NOPE], kr, zpad_a], axis=0)
        ka_ref[0, hd] = kfull.T.astype(BF16)
        va_ref[0, hd, 0] = jnp.concatenate([vh, ones_blk], axis=0).astype(BF16)

    cg, sg = cg_ref[...], sg_ref[...]
    hh = GQA_HEAD_DIM // 2
    qscale_b = LOG2E / math.sqrt(GQA_HEAD_DIM)
    zpad_b = jnp.zeros((HEAD_PAD - GQA_HEAD_DIM, tm), F32)

    def axial(t):
        return jnp.concatenate([_rope_rows(t[:hh], cg[:hh], sg[:hh]),
                                _rope_rows(t[hh:], cg[hh:], sg[hh:])], axis=0)

    gqn, gkn = gqn_ref[...], gkn_ref[...]
    grp = GQA_Q_HEADS // GQA_KV_HEADS
    for hd in range(GQA_Q_HEADS):
        qh = proj[o_qg + hd * GQA_HEAD_DIM:o_qg + (hd + 1) * GQA_HEAD_DIM]
        qh = axial(_rms_rows(qh, gqn)) * qscale_b
        parts = [qh, zpad_b] if hd // grp == 0 else [zpad_b, qh]
        qb_ref[0, hd, 0, 0] = jnp.concatenate(parts, axis=0).astype(BF16)
    kparts = []
    for g in range(GQA_KV_HEADS):
        kh = proj[o_kg + g * GQA_HEAD_DIM:o_kg + (g + 1) * GQA_HEAD_DIM]
        kparts.append(axial(_rms_rows(kh, gkn)))
        vh = proj[o_vg + g * GQA_HEAD_DIM:o_vg + (g + 1) * GQA_HEAD_DIM]
        vb_ref[0, g, 0] = jnp.concatenate([vh, ones_blk], axis=0).astype(BF16)
    kb_ref[0, 0] = jnp.concatenate(kparts, axis=0).T.astype(BF16)


def _col(v):
    return v.reshape(-1, 1).astype(F32)


def _rope_tables_t(pos, dim):
    inv = ROPE_THETA ** (-jnp.arange(0, dim, 2, dtype=F32) / dim)
    ang = pos.astype(F32)[:, None] * inv[None, :]
    ang = jnp.concatenate([ang, ang], axis=-1)
    sign = jnp.concatenate([-jnp.ones((dim // 2,), F32), jnp.ones((dim // 2,), F32)])
    return jnp.cos(ang).T, (jnp.sin(ang) * sign[None, :]).T


def _front_ab(x, mod, g_mix, w_in, q_a_norm, w_qb, kv_a_norm, w_kvb, mla_qn, mla_kn, gqa_qn, gqa_kn):
    b, s, d = x.shape
    tm = TOK_TILE
    nt = s // tm
    n_in = w_in.shape[1]
    pos = jnp.arange(s, dtype=I32)
    cm, sm = _rope_tables_t(pos, MLA_ROPE)
    half = GQA_HEAD_DIM // 2
    rc, rs = _rope_tables_t(pos // GRID_W, half)
    cc, cs = _rope_tables_t(pos % GRID_W, half)
    cg, sg = jnp.concatenate([rc, cc], axis=0), jnp.concatenate([rs, cs], axis=0)
    dve = MLA_V + ONES_ROWS
    tab = lambda rows: pl.BlockSpec((rows, tm), lambda bi, si: (0, si))
    outs = pl.pallas_call(
        _front_ab_kernel,
        out_shape=[
            jax.ShapeDtypeStruct((b, MLA_HEADS, 1, nt, HEAD_PAD, tm), BF16),
            jax.ShapeDtypeStruct((b, MLA_HEADS, s, HEAD_PAD), BF16),
            jax.ShapeDtypeStruct((b, MLA_HEADS, nt, dve, tm), BF16),
            jax.ShapeDtypeStruct((b, GQA_Q_HEADS, 1, nt, HEAD_PAD, tm), BF16),
            jax.ShapeDtypeStruct((b, 1, s, HEAD_PAD), BF16),
            jax.ShapeDtypeStruct((b, GQA_KV_HEADS, nt, dve, tm), BF16),
        ],
        grid=(b, nt),
        in_specs=[
            pl.BlockSpec((1, tm, d), lambda bi, si: (bi, si, 0)),
            pl.BlockSpec((1, 6, d), lambda bi, si: (bi, 0, 0)),
            _full((1, d)),
            _full((n_in, d)),
            _full((MLA_Q_LORA, 1)), _full((MLA_KV_LORA, 1)),
            _full((w_qb.shape[1], MLA_Q_LORA)), _full((w_kvb.shape[1], MLA_KV_LORA)),
            _full((MLA_NOPE + MLA_ROPE, 1)), _full((MLA_NOPE + MLA_ROPE, 1)),
            _full((GQA_HEAD_DIM, 1)), _full((GQA_HEAD_DIM, 1)),
            tab(MLA_ROPE), tab(MLA_ROPE), tab(GQA_HEAD_DIM), tab(GQA_HEAD_DIM),
        ],
        out_specs=[
            pl.BlockSpec((1, MLA_HEADS, 1, 1, HEAD_PAD, tm), lambda bi, si: (bi, 0, 0, si, 0, 0)),
            pl.BlockSpec((1, MLA_HEADS, tm, HEAD_PAD), lambda bi, si: (bi, 0, si, 0)),
            pl.BlockSpec((1, MLA_HEADS, 1, dve, tm), lambda bi, si: (bi, 0, si, 0, 0)),
            pl.BlockSpec((1, GQA_Q_HEADS, 1, 1, HEAD_PAD, tm), lambda bi, si: (bi, 0, 0, si, 0, 0)),
            pl.BlockSpec((1, 1, tm, HEAD_PAD), lambda bi, si: (bi, 0, si, 0)),
            pl.BlockSpec((1, GQA_KV_HEADS, 1, dve, tm), lambda bi, si: (bi, 0, si, 0, 0)),
        ],
        compiler_params=_cparams(("parallel", "parallel")),
        name="front_ab",
    )(x, mod, g_mix.reshape(1, d), w_in.T.astype(BF16), _col(q_a_norm), _col(kv_a_norm),
      w_qb.T.astype(BF16), w_kvb.T.astype(BF16), _col(mla_qn), _col(mla_kn), _col(gqa_qn), _col(gqa_kn),
      cm, sm, cg, sg)
    return outs


def _front_c_kernel(x_ref, mod_ref, gmix_ref, wint_ref, qn_ref, kn_ref, q_ref, k_ref, v_ref):
    tm = x_ref.shape[1]
    x = x_ref[0]
    shift, scale = mod_ref[0, 0:1, :], mod_ref[0, 1:2, :]
    h = _rms_tok(x, gmix_ref[...]) * (1.0 + scale) + shift
    proj = lax.dot_general(wint_ref[...], h.astype(BF16), _NT, preferred_element_type=F32)
    nq = DIFF_HEADS * 2 * DIFF_QK
    qn, kn = qn_ref[...], kn_ref[...]
    qscale = LOG2E / math.sqrt(DIFF_QK)
    zpad = jnp.zeros((DIFF_QK, tm), F32)
    ones_blk = _ones_block(tm)
    for hd in range(DIFF_HEADS):
        kparts = []
        for c in range(2):
            r0 = (hd * 2 + c) * DIFF_QK
            qc = _rms_rows(proj[r0:r0 + DIFF_QK], qn[c * DIFF_QK:(c + 1) * DIFF_QK]) * qscale
            parts = [qc, zpad] if c == 0 else [zpad, qc]
            q_ref[0, hd, c, 0] = jnp.concatenate(parts, axis=0).astype(BF16)
            kparts.append(_rms_rows(proj[nq + r0:nq + r0 + DIFF_QK], kn[c * DIFF_QK:(c + 1) * DIFF_QK]))
        k_ref[0, hd] = jnp.concatenate(kparts, axis=0).T.astype(BF16)
        vh = proj[2 * nq + hd * DIFF_V:2 * nq + (hd + 1) * DIFF_V]
        v_ref[0, hd, 0] = jnp.concatenate([vh, ones_blk], axis=0).astype(BF16)


def _front_c(x, mod, g_mix, w_in, qn, kn):
    b, s, d = x.shape
    tm = TOK_TILE
    nt = s // tm
    dve = DIFF_V + ONES_ROWS
    return pl.pallas_call(
        _front_c_kernel,
        out_shape=[
            jax.ShapeDtypeStruct((b, DIFF_HEADS, 2, nt, HEAD_PAD, tm), BF16),
            jax.ShapeDtypeStruct((b, DIFF_HEADS, s, HEAD_PAD), BF16),
            jax.ShapeDtypeStruct((b, DIFF_HEADS, nt, dve, tm), BF16),
        ],
        grid=(b, nt),
        in_specs=[
            pl.BlockSpec((1, tm, d), lambda bi, si: (bi, si, 0)),
            pl.BlockSpec((1, 6, d), lambda bi, si: (bi, 0, 0)),
            _full((1, d)),
            _full((w_in.shape[1], d)),
            _full((2 * DIFF_QK, 1)), _full((2 * DIFF_QK, 1)),
        ],
        out_specs=[
            pl.BlockSpec((1, DIFF_HEADS, 2, 1, HEAD_PAD, tm), lambda bi, si: (bi, 0, 0, si, 0, 0)),
            pl.BlockSpec((1, DIFF_HEADS, tm, HEAD_PAD), lambda bi, si: (bi, 0, si, 0)),
            pl.BlockSpec((1, DIFF_HEADS, 1, dve, tm), lambda bi, si: (bi, 0, si, 0, 0)),
        ],
        compiler_params=_cparams(("parallel", "parallel")),
        name="front_c",
    )(x, mod, g_mix.reshape(1, d), w_in.T.astype(BF16), _col(qn), _col(kn))


def _attn_kernel(*refs, ncomp, dv, tk, nk, nq, nbias, bias_step, bias_lo, diff_scale):
    has_bias = nbias > 0
    if has_bias:
        lam_ref, q_ref, k_ref, v_ref, bias_ref, subln_ref, o_ref, m_sc, acc_sc, cmax_sc, *s_bufs = refs
    else:
        q_ref, k_ref, v_ref, o_ref, m_sc, acc_sc, cmax_sc, *s_bufs = refs
    tq = q_ref.shape[-1]
    m_sc[...] = jnp.full(m_sc.shape, NEG_BIG, F32)
    acc_sc[...] = jnp.zeros(acc_sc.shape, F32)

    per_iter = len(s_bufs)
    qper = per_iter // ncomp
    ipj = nq // qper
    n_iter = nk * ipj

    def where(i, k):
        j = jnp.minimum(i // ipj, nk - 1)
        return j, (i % ipj) * qper + k // ncomp, k % ncomp

    def scores(jqc, slot):
        j, qi, c = jqc
        k_t = k_ref[0, 0, pl.ds(pl.multiple_of(j * tk, tk), tk), :]
        s = jnp.dot(k_t, q_ref[0, 0, c, qi], preferred_element_type=F32)
        if has_bias:
            d = j * tk - qi * tq
            bidx = jnp.clip((d - bias_lo) // bias_step, 0, nbias - 1)
            s = s + bias_ref[0, bidx]
        s_bufs[slot][...] = s
        cmax_sc[slot] = jnp.max(s, axis=0, keepdims=True)

    def consume(slot, jqc):
        j, qi, c = jqc
        m_prev = m_sc[c, qi]
        m_new = jnp.maximum(m_prev, cmax_sc[slot])
        m_sc[c, qi] = m_new
        alpha = jnp.exp2(m_prev - m_new)
        p = jnp.exp2(s_bufs[slot][...] - m_new).astype(BF16)
        acc_sc[c, qi] = alpha * acc_sc[c, qi] + jnp.dot(v_ref[0, 0, j], p, preferred_element_type=F32)

    half = 2
    for k in range(half):
        scores(where(0, k), k)

    def body(i, carry):
        for k in range(per_iter):
            consume(k, where(i, k))
            nk_ = k + half
            scores(where(i, nk_) if nk_ < per_iter else where(i + 1, nk_ - per_iter), nk_ % per_iter)
        return carry

    lax.fori_loop(0, n_iter, body, 0)

    for qi in range(nq):
        cols = slice(qi * tq, (qi + 1) * tq)
        if ncomp == 1:
            acc = acc_sc[0, qi]
            o_ref[0, :, cols] = (acc[:dv] / acc[dv:dv + 1]).astype(o_ref.dtype)
        else:
            a0, a1 = acc_sc[0, qi], acc_sc[1, qi]
            o = a0[:dv] / a0[dv:dv + 1] - lam_ref[0] * (a1[:dv] / a1[dv:dv + 1])
            o = _rms_rows(o, subln_ref[...]) * diff_scale
            o_ref[0, :, cols] = o.astype(o_ref.dtype)


def _attention(q_t, k, v_t, *, dv, k_div, v_div, bias=None, bias_step=0, bias_lo=0, lam=None, subln=None,
               diff_scale=1.0):
    b, hq, ncomp, nq, dpad, tq = q_t.shape
    nk, dve, tk = v_t.shape[2:]
    s = nq * tq
    nbias = 0 if bias is None else bias.shape[1]
    in_specs = [
        pl.BlockSpec((1, 1, ncomp, nq, dpad, tq), lambda bi, hi: (bi, hi, 0, 0, 0, 0)),
        pl.BlockSpec((1, 1, s, dpad), lambda bi, hi: (bi, hi // k_div, 0, 0)),
        pl.BlockSpec((1, 1, nk, dve, tk), lambda bi, hi: (bi, hi // v_div, 0, 0, 0)),
    ]
    args = [q_t, k, v_t]
    if bias is not None:
        in_specs = [pl.BlockSpec(memory_space=pltpu.SMEM)] + in_specs + [
            pl.BlockSpec((1, nbias, tk, tq), lambda bi, hi: (hi, 0, 0, 0)),
            _full((dv, 1)),
        ]
        args = [lam] + args + [bias, subln]
    kern = functools.partial(_attn_kernel, ncomp=ncomp, dv=dv, tk=tk, nk=nk, nq=nq, nbias=nbias,
                             bias_step=bias_step, bias_lo=bias_lo, diff_scale=diff_scale)
    return pl.pallas_call(
        kern,
        out_shape=jax.ShapeDtypeStruct((b, hq * dv, s), BF16),
        grid=(b, hq),
        in_specs=in_specs,
        out_specs=pl.BlockSpec((1, dv, s), lambda bi, hi: (bi, hi, 0)),
        scratch_shapes=[pltpu.VMEM((ncomp, nq, 1, tq), F32), pltpu.VMEM((ncomp, nq, dve, tq), F32),
                        pltpu.VMEM((ATTN_ITEMS, 1, tq), F32)] + [pltpu.VMEM((tk, tq), F32)] * ATTN_ITEMS,
        compiler_params=pltpu.CompilerParams(dimension_semantics=("parallel", "parallel"),
                                             vmem_limit_bytes=ATTN_VMEM_LIMIT),
        name="attn_diff" if ncomp == 2 else "attn_softmax",
    )(*args)


def _t5_bucket(rel):
    nb = REL_BUCKETS // 2
    max_exact = nb // 2
    ret = jnp.where(rel > 0, nb, 0)
    n = jnp.abs(rel)
    nf = jnp.maximum(n, 1).astype(F32)
    large = max_exact + (jnp.log(nf / max_exact) / math.log(REL_MAX_DIST / max_exact) * (nb - max_exact)).astype(I32)
    large = jnp.minimum(large, nb - 1)
    return ret + jnp.where(n < max_exact, n, large)


def _bias_tiles(rel_bias, tk, tq):
    step = math.gcd(tk, tq)
    lo = -(-(REL_MAX_DIST - 1 + tk) // step) * step
    hi = -(-(REL_MAX_DIST - 1 + tq) // step) * step
    lo = -lo
    offs = jnp.arange(lo, hi + 1, step, dtype=I32)
    rel = offs[:, None, None] + jnp.arange(tk, dtype=I32)[None, :, None] - jnp.arange(tq, dtype=I32)[None, None, :]
    bucket = _t5_bucket(rel)[None]
    rb = rel_bias.astype(F32) * LOG2E
    tiles = jnp.zeros((rel_bias.shape[1],) + rel.shape, F32)
    for bkt in range(REL_BUCKETS):
        tiles = jnp.where(bucket == bkt, rb[bkt][:, None, None, None], tiles)
    return tiles, step, lo


def _post_kernel(*refs, n_o):
    o_refs = refs[:n_o]
    (woutt_ref, x_ref, mod_ref, gffn_ref, wrt_ref, br_ref,
     x1_ref, h2_ref, ri_ref, rf_ref, cnt_ref, carry_sc) = refs[n_o:]
    tm = x_ref.shape[1]
    first = jnp.logical_and(pl.program_id(0) == 0, pl.program_id(1) == 0)

    @pl.when(first)
    def _():
        carry_sc[...] = jnp.zeros(carry_sc.shape, F32)

    off = 0
    yt = None
    for o_ref in o_refs:
        n = o_ref.shape[1]
        part = jnp.dot(woutt_ref[:, off:off + n], o_ref[0], preferred_element_type=F32)
        yt = part if yt is None else yt + part
        off += n
    y = yt.T
    gate1 = mod_ref[0, 2:3, :]
    x1 = x_ref[0] + gate1 * y
    x1_ref[0] = x1
    shift2, scale2 = mod_ref[0, 3:4, :], mod_ref[0, 4:5, :]
    h2 = _rms_tok(x1, gffn_ref[...]) * (1.0 + scale2) + shift2
    h2_ref[0] = h2

    lg = lax.dot_general(wrt_ref[...], h2, _NT, preferred_element_type=F32,
                         precision=lax.Precision.HIGHEST) + br_ref[...]
    g_log = lg[0:MOE_GROUPS]
    gmax = jnp.max(g_log, axis=0, keepdims=True)
    iota_g = lax.broadcasted_iota(I32, (MOE_GROUPS, tm), 0)
    g_idx = jnp.min(jnp.where(g_log == gmax, iota_g, MOE_GROUPS), axis=0, keepdims=True)
    pg_top = 1.0 / jnp.sum(jnp.exp(g_log - gmax), axis=0, keepdims=True)
    el = jnp.zeros((MOE_EPG, tm), F32)
    for g in range(MOE_GROUPS):
        el = el + jnp.where(g_idx == g, lg[MOE_GROUPS + g * MOE_EPG:MOE_GROUPS + (g + 1) * MOE_EPG], 0.0)
    iota_e = lax.broadcasted_iota(I32, (MOE_EPG, tm), 0)
    e_max = jnp.max(el, axis=0, keepdims=True)
    i0 = jnp.min(jnp.where(el == e_max, iota_e, MOE_EPG), axis=0, keepdims=True)
    el2 = jnp.where(iota_e == i0, -jnp.inf, el)
    e_max2 = jnp.max(el2, axis=0, keepdims=True)
    i1 = jnp.min(jnp.where(el2 == e_max2, iota_e, MOE_EPG), axis=0, keepdims=True)
    p1 = jnp.exp(e_max2 - e_max)
    w0 = pg_top / (1.0 + p1)
    w1 = pg_top * p1 / (1.0 + p1)
    e0 = g_idx * MOE_EPG + i0
    e1 = g_idx * MOE_EPG + i1

    iota_x = lax.broadcasted_iota(I32, (MOE_EXPERTS, tm), 0)
    hit0, hit1 = iota_x == e0, iota_x == e1
    oh = jnp.logical_or(hit0, hit1).astype(BF16)
    upper = (lax.broadcasted_iota(I32, (tm, tm), 0) < lax.broadcasted_iota(I32, (tm, tm), 1)).astype(BF16)
    prefix = jnp.dot(oh, upper, preferred_element_type=F32) + carry_sc[:, 0:1]
    rank0 = jnp.sum(jnp.where(hit0, prefix, 0.0), axis=0, keepdims=True)
    rank1 = jnp.sum(jnp.where(hit1, prefix, 0.0), axis=0, keepdims=True)
    total = carry_sc[...] + jnp.sum(oh.astype(F32), axis=1, keepdims=True)
    carry_sc[...] = total
    cnt_ref[...] = total.astype(I32)

    zi = jnp.zeros((4, tm), I32)
    ri_ref[0] = jnp.concatenate([e0, e1, rank0.astype(I32), rank1.astype(I32), zi], axis=0)
    rf_ref[0] = jnp.concatenate([w0, w1, jnp.zeros((6, tm), F32)], axis=0)


def _post(o_list, w_out, x, mod, g_ffn, w_group, b_group, w_expert, b_expert):
    b, s, d = x.shape
    tm = TOK_TILE
    nt = s // tm
    n_r = MOE_GROUPS + MOE_EXPERTS
    n_rp = 40
    w_r = jnp.concatenate([w_group.T, jnp.transpose(w_expert, (0, 2, 1)).reshape(MOE_EXPERTS, d),
                           jnp.zeros((n_rp - n_r, d), F32)], axis=0).astype(F32)
    b_r = jnp.concatenate([b_group, b_expert.reshape(-1), jnp.zeros((n_rp - n_r,), F32)]).reshape(n_rp, 1)
    n_o = len(o_list)
    in_specs = [pl.BlockSpec((1, o.shape[1], tm), lambda bi, si: (bi, 0, si)) for o in o_list] + [
        _full((d, w_out.shape[0])),
        pl.BlockSpec((1, tm, d), lambda bi, si: (bi, si, 0)),
        pl.BlockSpec((1, 6, d), lambda bi, si: (bi, 0, 0)),
        _full((1, d)),
        _full((n_rp, d)), _full((n_rp, 1)),
    ]
    return pl.pallas_call(
        functools.partial(_post_kernel, n_o=n_o),
        out_shape=[
            jax.ShapeDtypeStruct((b, s, d), F32),
            jax.ShapeDtypeStruct((b, s, d), F32),
            jax.ShapeDtypeStruct((b, 8, s), I32),
            jax.ShapeDtypeStruct((b, 8, s), F32),
            jax.ShapeDtypeStruct((MOE_EXPERTS, 128), I32),
        ],
        grid=(b, nt),
        in_specs=in_specs,
        out_specs=[
            pl.BlockSpec((1, tm, d), lambda bi, si: (bi, si, 0)),
            pl.BlockSpec((1, tm, d), lambda bi, si: (bi, si, 0)),
            pl.BlockSpec((1, 8, tm), lambda bi, si: (bi, 0, si)),
            pl.BlockSpec((1, 8, tm), lambda bi, si: (bi, 0, si)),
            _full((MOE_EXPERTS, 128)),
        ],
        scratch_shapes=[pltpu.VMEM((MOE_EXPERTS, 128), F32)],
        compiler_params=_cparams(("arbitrary", "arbitrary")),
        name="post_router",
    )(*o_list, w_out.T.astype(BF16), x, mod, g_ffn.reshape(1, d), w_r, b_r)


def _row_copy(src_ref, row, dst_ref, slot, sem):
    return pltpu.make_async_copy(src_ref.at[pl.ds(row, 1)], dst_ref.at[pl.ds(slot, 1)], sem)


def _expert_kernel(be_ref, nu_ref, cur_ref, nxt_ref, h2_ref, wup_ref, wdn_ref, ys_ref, buf, sem):
    i = pl.program_id(0)
    bm = buf.shape[1]
    n_used = nu_ref[0]
    slot = i % 2

    def issue(idx_ref, sl):
        def body(r, carry):
            _row_copy(h2_ref, idx_ref[0, 0, r], buf.at[sl], r, sem.at[sl]).start()
            return carry
        lax.fori_loop(0, bm, body, 0, unroll=8)

    @pl.when(i == 0)
    def _():
        issue(cur_ref, 0)

    @pl.when(i + 1 < n_used)
    def _():
        issue(nxt_ref, 1 - slot)

    @pl.when(i < n_used)
    def _():
        pltpu.make_async_copy(h2_ref.at[pl.ds(0, bm)], buf.at[slot], sem.at[slot]).wait()
        f = wdn_ref.shape[1]
        xb = buf[slot].astype(BF16)
        gu = jnp.dot(xb, wup_ref[0], preferred_element_type=F32)
        g, u = gu[:, :f], gu[:, f:]
        a = (g / (1.0 + jnp.exp(-g))) * u
        ys_ref[...] = jnp.dot(a.astype(BF16), wdn_ref[0], preferred_element_type=F32)

    @pl.when(i >= n_used)
    def _():
        ys_ref[...] = jnp.zeros(ys_ref.shape, F32)


def _experts(h2, slot_t, block_e, n_used, w_up, w_down):
    t, d = h2.shape
    bm = MOE_BLOCK
    n_blocks = slot_t.shape[0] // bm
    e, _, f2 = w_up.shape
    f = f2 // 2
    idx = slot_t.reshape(n_blocks, 1, bm)
    grid_spec = pltpu.PrefetchScalarGridSpec(
        num_scalar_prefetch=2,
        grid=(n_blocks,),
        in_specs=[
            pl.BlockSpec((1, 1, bm), lambda i, be, nu: (i, 0, 0), memory_space=pltpu.SMEM),
            pl.BlockSpec((1, 1, bm), lambda i, be, nu: (jnp.minimum(i + 1, n_blocks - 1), 0, 0),
                         memory_space=pltpu.SMEM),
            pl.BlockSpec(memory_space=pl.ANY),
            pl.BlockSpec((1, d, f2), lambda i, be, nu: (be[i], 0, 0)),
            pl.BlockSpec((1, f, d), lambda i, be, nu: (be[i], 0, 0)),
        ],
        out_specs=pl.BlockSpec((bm, d), lambda i, be, nu: (i, 0)),
        scratch_shapes=[pltpu.VMEM((2, bm, d), F32), pltpu.SemaphoreType.DMA((2,))],
    )
    return pl.pallas_call(
        _expert_kernel,
        out_shape=jax.ShapeDtypeStruct((n_blocks * bm, d), F32),
        grid_spec=grid_spec,
        compiler_params=_cparams(("arbitrary",)),
        name="moe_experts",
    )(block_e, n_used, idx, idx, h2, w_up.astype(BF16), w_down.astype(BF16))


def _combine_kernel(dst_ref, x_ref, mod_ref, gw_ref, ys_ref, o_ref, buf, sem):
    tc = x_ref.shape[0]
    for k in range(MOE_TOP_K):
        def body(r, carry, k=k):
            _row_copy(ys_ref, dst_ref[0, k, r], buf.at[k], r, sem.at[k]).start()
            return carry
        lax.fori_loop(0, tc, body, 0, unroll=8)
    acc = None
    for k in range(MOE_TOP_K):
        pltpu.make_async_copy(ys_ref.at[pl.ds(0, tc)], buf.at[k], sem.at[k]).wait()
        term = buf[k] * gw_ref[:, k:k + 1]
        acc = term if acc is None else acc + term
    o_ref[...] = x_ref[...] + mod_ref[0, 5:6, :] * acc


def _combine(x1, mod, dest, gates, ys):
    t, d = x1.shape
    tc = COMBINE_TILE
    b = mod.shape[0]
    per_b = t // b // tc
    return pl.pallas_call(
        _combine_kernel,
        out_shape=jax.ShapeDtypeStruct((t, d), F32),
        grid=(t // tc,),
        in_specs=[
            pl.BlockSpec((1, MOE_TOP_K, tc), lambda i: (i, 0, 0), memory_space=pltpu.SMEM),
            pl.BlockSpec((tc, d), lambda i: (i, 0)),
            pl.BlockSpec((1, 6, d), lambda i: (i // per_b, 0, 0)),
            pl.BlockSpec((tc, MOE_TOP_K), lambda i: (i, 0)),
            pl.BlockSpec(memory_space=pl.ANY),
        ],
        out_specs=pl.BlockSpec((tc, d), lambda i: (i, 0)),
        scratch_shapes=[pltpu.VMEM((MOE_TOP_K, tc, d), F32), pltpu.SemaphoreType.DMA((MOE_TOP_K,))],
        compiler_params=_cparams(("arbitrary",)),
        name="moe_combine",
    )(dest, x1, mod, gates, ys)


def _moe(x1, h2, ri, rf, counts, mod, w_up, w_down):
    b, s, d = x1.shape
    t = b * s
    bm = MOE_BLOCK
    n_blocks = (t * MOE_TOP_K) // bm + MOE_EXPERTS
    cnt = counts[:, 0]
    padded = (cnt + bm - 1) // bm * bm
    pad_end = jnp.cumsum(padded)
    pad_start = pad_end - padded
    n_used = (pad_end[-1] // bm).astype(I32).reshape(1)
    blk0 = jnp.arange(n_blocks, dtype=I32) * bm
    block_e = jnp.minimum(jnp.sum((pad_end[None, :] <= blk0[:, None]).astype(I32), axis=1), MOE_EXPERTS - 1)
    e = ri[:, 0:2, :]
    dest = ri[:, 2:4, :]
    for ex in range(MOE_EXPERTS):
        dest = dest + jnp.where(e == ex, pad_start[ex], 0)
    tok = jnp.broadcast_to(jnp.arange(t, dtype=I32).reshape(b, 1, s), (b, MOE_TOP_K, s))
    slot_t = jnp.zeros((n_blocks * bm,), I32).at[dest.reshape(-1)].set(tok.reshape(-1))
    ys = _experts(h2.reshape(t, d), slot_t, block_e, n_used, w_up, w_down)
    tc = COMBINE_TILE
    dest_c = dest.reshape(b, MOE_TOP_K, s // tc, tc).transpose(0, 2, 1, 3).reshape(t // tc, MOE_TOP_K, tc)
    gates = rf[:, 0:2, :].transpose(0, 2, 1).reshape(t, MOE_TOP_K)
    out = _combine(x1.reshape(t, d), mod, dest_c, gates, ys)
    return out.reshape(b, s, d)


def kernel(x, c, ada_w, ada_b, norm_mix, norm_ffn, rel_bias, ab_w_in, ab_q_a_norm, ab_w_qb, ab_kv_a_norm, ab_w_kvb, ab_mla_qn, ab_mla_kn, ab_gqa_qn, ab_gqa_kn, ab_w_out, c_w_in, c_qn, c_kn, c_lam_q1, c_lam_k1, c_lam_q2, c_lam_k2, c_subln, c_w_out, moe_w_group, moe_b_group, moe_w_expert, moe_b_expert, moe_w_up, moe_w_down):
    depth = ada_w.shape[0]
    mod_all = _ada_mod(c, ada_w, ada_b)
    for l in range(depth):
        mod = mod_all[l]
        i = l // 2
        if l % 2 == 0:
            qa, ka, va, qb, kb, vb = _front_ab(x, mod, norm_mix[l], ab_w_in[i], ab_q_a_norm[i], ab_w_qb[i],
                                               ab_kv_a_norm[i], ab_w_kvb[i], ab_mla_qn[i], ab_mla_kn[i],
                                               ab_gqa_qn[i], ab_gqa_kn[i])
            o_a = _attention(qa, ka, va, dv=MLA_V, k_div=1, v_div=1)
            o_b = _attention(qb, kb, vb, dv=GQA_HEAD_DIM, k_div=GQA_Q_HEADS, v_div=GQA_Q_HEADS // GQA_KV_HEADS)
            o_list, w_out = [o_a, o_b], ab_w_out[i]
        else:
            lambda_init = 0.8 - 0.6 * math.exp(-0.3 * l)
            q, k, v = _front_c(x, mod, norm_mix[l], c_w_in[i], c_qn[i], c_kn[i])
            lam = (jnp.exp(jnp.sum(c_lam_q1[i].astype(F32) * c_lam_k1[i].astype(F32)))
                   - jnp.exp(jnp.sum(c_lam_q2[i].astype(F32) * c_lam_k2[i].astype(F32))) + lambda_init)
            tiles, step, lo = _bias_tiles(rel_bias, TOK_TILE, TOK_TILE)
            o_c = _attention(q, k, v, dv=DIFF_V, k_div=1, v_div=1, bias=tiles, bias_step=step, bias_lo=lo,
                             lam=lam.reshape(1).astype(F32), subln=_col(c_subln[i]),
                             diff_scale=1.0 - lambda_init)
            o_list, w_out = [o_c], c_w_out[i]
        x1, h2, ri, rf, counts = _post(o_list, w_out, x, mod, norm_ffn[l], moe_w_group[l], moe_b_group[l],
                                       moe_w_expert[l], moe_b_expert[l])
        x = _moe(x1, h2, ri, rf, counts, mod, moe_w_up[l], moe_w_down[l])
    return x
```

```python
import functools
import math

import jax
import jax.numpy as jnp
from jax import lax
from jax.experimental import pallas as pl
from jax.experimental.pallas import tpu as pltpu

F32 = jnp.float32
BF16 = jnp.bfloat16
I32 = jnp.int32

EPS = 1e-6
ROPE_THETA = 10000.0
GRID_W = 64
LOG2E = 1.4426950408889634
NEG_BIG = -1e30

MLA_HEADS, MLA_Q_LORA, MLA_KV_LORA, MLA_NOPE, MLA_ROPE, MLA_V = 8, 512, 256, 64, 32, 64
GQA_Q_HEADS, GQA_KV_HEADS, GQA_HEAD_DIM = 8, 2, 64
DIFF_HEADS, DIFF_QK = 8, 64
DIFF_V = 2 * DIFF_QK
REL_BUCKETS, REL_MAX_DIST = 32, 128
MOE_GROUPS, MOE_EPG, MOE_TOP_K = 4, 8, 2
MOE_EXPERTS = MOE_GROUPS * MOE_EPG

HEAD_PAD = 128
ONES_ROWS = 16
VMEM_LIMIT = 48 * 1024 * 1024
ATTN_VMEM_LIMIT = 56 * 1024 * 1024
ATTN_ITEMS = 4

TOK_TILE = 512
MOE_BLOCK = 256
COMBINE_TILE = 256
DISPATCH_TILE = 1024

_NT = (((1,), (1,)), ((), ()))


def _cparams(sem):
    return pltpu.CompilerParams(dimension_semantics=sem, vmem_limit_bytes=VMEM_LIMIT)


def _full(shape):
    n = len(shape)
    return pl.BlockSpec(shape, lambda *_: (0,) * n)


def _rms_rows(xt, gcol):
    ms = jnp.mean(xt * xt, axis=0, keepdims=True)
    return xt * lax.rsqrt(ms + EPS) * gcol


def _rms_tok(x, grow):
    ms = jnp.mean(x * x, axis=-1, keepdims=True)
    return x * lax.rsqrt(ms + EPS) * grow


def _rope_rows(x, cos, sin_signed):
    half = x.shape[0] // 2
    sw = jnp.concatenate([x[half:], x[:half]], axis=0)
    return x * cos + sw * sin_signed


def _ones_block(tm):
    return (lax.broadcasted_iota(I32, (ONES_ROWS, tm), 0) == 0).astype(F32)


def _ada_kernel(c_ref, w_ref, b_ref, o_ref):
    c = c_ref[...]
    cond = c / (1.0 + jnp.exp(-c))
    o_ref[0] = jnp.dot(cond, w_ref[0], preferred_element_type=F32,
                       precision=lax.Precision.HIGHEST) + b_ref[0]


def _ada_mod(c, ada_w, ada_b):
    depth, d, n6 = ada_w.shape
    b = c.shape[0]
    rows = 8
    tn = 1536
    cp = jnp.zeros((rows, d), F32).at[:b].set(c)
    out = pl.pallas_call(
        _ada_kernel,
        out_shape=jax.ShapeDtypeStruct((depth, rows, n6), F32),
        grid=(depth, n6 // tn),
        in_specs=[_full((rows, d)),
                  pl.BlockSpec((1, d, tn), lambda l, j: (l, 0, j)),
                  pl.BlockSpec((1, 1, tn), lambda l, j: (l, 0, j))],
        out_specs=pl.BlockSpec((1, rows, tn), lambda l, j: (l, 0, j)),
        compiler_params=_cparams(("arbitrary", "arbitrary")),
        name="ada_mod",
    )(cp, ada_w, ada_b.reshape(depth, 1, n6))
    return out[:, :b].reshape(depth, b, 6, d)


def _front_ab_kernel(x_ref, mod_ref, gmix_ref, wint_ref, qan_ref, kvan_ref, wqbt_ref, wkvbt_ref,
                     mqn_ref, mkn_ref, gqn_ref, gkn_ref, cm_ref, sm_ref, cg_ref, sg_ref,
                     qa_ref, ka_ref, va_ref, qb_ref, kb_ref, vb_ref):
    tm = x_ref.shape[1]
    x = x_ref[0]
    shift, scale = mod_ref[0, 0:1, :], mod_ref[0, 1:2, :]
    h = _rms_tok(x, gmix_ref[...]) * (1.0 + scale) + shift
    proj = lax.dot_general(wint_ref[...], h.astype(BF16), _NT, preferred_element_type=F32)

    o_ckv = MLA_Q_LORA
    o_kpe = o_ckv + MLA_KV_LORA
    o_qg = o_kpe + MLA_ROPE
    o_kg = o_qg + GQA_Q_HEADS * GQA_HEAD_DIM
    o_vg = o_kg + GQA_KV_HEADS * GQA_HEAD_DIM
    cq = _rms_rows(proj[0:o_ckv], qan_ref[...]).astype(BF16)
    ckv = _rms_rows(proj[o_ckv:o_kpe], kvan_ref[...]).astype(BF16)
    kpe = proj[o_kpe:o_qg]
    q_all = jnp.dot(wqbt_ref[...], cq, preferred_element_type=F32)
    kv_all = jnp.dot(wkvbt_ref[...], ckv, preferred_element_type=F32)

    cm, sm = cm_ref[...], sm_ref[...]
    ones_blk = _ones_block(tm)
    dq = MLA_NOPE + MLA_ROPE
    zpad_a = jnp.zeros((HEAD_PAD - dq, tm), F32)
    qscale_a = LOG2E / math.sqrt(dq)
    mqn, mkn = mqn_ref[...], mkn_ref[...]
    kpe_ss = jnp.sum(kpe * kpe, axis=0, keepdims=True)
    for hd in range(MLA_HEADS):
        qh = _rms_rows(q_all[hd * dq:(hd + 1) * dq], mqn)
        qr = _rope_rows(qh[MLA_NOPE:], cm, sm)
        qfull = jnp.concatenate([qh[:MLA_NOPE], qr, zpad_a], axis=0) * qscale_a
        qa_ref[0, hd, 0, 0] = qfull.astype(BF16)

        base = hd * (MLA_NOPE + MLA_V)
        kn = kv_all[base:base + MLA_NOPE]
        vh = kv_all[base + MLA_NOPE:base + MLA_NOPE + MLA_V]
        ms = (jnp.sum(kn * kn, axis=0, keepdims=True) + kpe_ss) * (1.0 / dq)
        r = lax.rsqrt(ms + EPS)
        kr = _rope_rows(kpe * r * mkn[MLA_NOPE:], cm, sm)
        kfull = jnp.concatenate([kn * r * mkn[:MLA_NOPE], kr, zpad_a], axis=0)
        ka_ref[0, hd] = kfull.T.astype(BF16)
        va_ref[0, hd, 0] = jnp.concatenate([vh, ones_blk], axis=0).astype(BF16)

    cg, sg = cg_ref[...], sg_ref[...]
    hh = GQA_HEAD_DIM // 2
    qscale_b = LOG2E / math.sqrt(GQA_HEAD_DIM)
    zpad_b = jnp.zeros((HEAD_PAD - GQA_HEAD_DIM, tm), F32)

    def axial(t):
        return jnp.concatenate([_rope_rows(t[:hh], cg[:hh], sg[:hh]),
                                _rope_rows(t[hh:], cg[hh:], sg[hh:])], axis=0)

    gqn, gkn = gqn_ref[...], gkn_ref[...]
    grp = GQA_Q_HEADS // GQA_KV_HEADS
    for hd in range(GQA_Q_HEADS):
        qh = proj[o_qg + hd * GQA_HEAD_DIM:o_qg + (hd + 1) * GQA_HEAD_DIM]
        qh = axial(_rms_rows(qh, gqn)) * qscale_b
        parts = [qh, zpad_b] if hd // grp == 0 else [zpad_b, qh]
        qb_ref[0, hd, 0, 0] = jnp.concatenate(parts, axis=0).astype(BF16)
    kparts = []
    for g in range(GQA_KV_HEADS):
        kh = proj[o_kg + g * GQA_HEAD_DIM:o_kg + (g + 1) * GQA_HEAD_DIM]
        kparts.append(axial(_rms_rows(kh, gkn)))
        vh = proj[o_vg + g * GQA_HEAD_DIM:o_vg + (g + 1) * GQA_HEAD_DIM]
        vb_ref[0, g, 0] = jnp.concatenate([vh, ones_blk], axis=0).astype(BF16)
    kb_ref[0, 0] = jnp.concatenate(kparts, axis=0).T.astype(BF16)


def _col(v):
    return v.reshape(-1, 1).astype(F32)


def _rope_tables_t(pos, dim):
    inv = ROPE_THETA ** (-jnp.arange(0, dim, 2, dtype=F32) / dim)
    ang = pos.astype(F32)[:, None] * inv[None, :]
    ang = jnp.concatenate([ang, ang], axis=-1)
    sign = jnp.concatenate([-jnp.ones((dim // 2,), F32), jnp.ones((dim // 2,), F32)])
    return jnp.cos(ang).T, (jnp.sin(ang) * sign[None, :]).T


def _front_ab(x, mod, g_mix, w_in, q_a_norm, w_qb, kv_a_norm, w_kvb, mla_qn, mla_kn, gqa_qn, gqa_kn):
    b, s, d = x.shape
    tm = TOK_TILE
    nt = s // tm
    n_in = w_in.shape[1]
    pos = jnp.arange(s, dtype=I32)
    cm, sm = _rope_tables_t(pos, MLA_ROPE)
    half = GQA_HEAD_DIM // 2
    rc, rs = _rope_tables_t(pos // GRID_W, half)
    cc, cs = _rope_tables_t(pos % GRID_W, half)
    cg, sg = jnp.concatenate([rc, cc], axis=0), jnp.concatenate([rs, cs], axis=0)
    dve = MLA_V + ONES_ROWS
    tab = lambda rows: pl.BlockSpec((rows, tm), lambda bi, si: (0, si))
    outs = pl.pallas_call(
        _front_ab_kernel,
        out_shape=[
            jax.ShapeDtypeStruct((b, MLA_HEADS, 1, nt, HEAD_PAD, tm), BF16),
            jax.ShapeDtypeStruct((b, MLA_HEADS, s, HEAD_PAD), BF16),
            jax.ShapeDtypeStruct((b, MLA_HEADS, nt, dve, tm), BF16),
            jax.ShapeDtypeStruct((b, GQA_Q_HEADS, 1, nt, HEAD_PAD, tm), BF16),
            jax.ShapeDtypeStruct((b, 1, s, HEAD_PAD), BF16),
            jax.ShapeDtypeStruct((b, GQA_KV_HEADS, nt, dve, tm), BF16),
        ],
        grid=(b, nt),
        in_specs=[
            pl.BlockSpec((1, tm, d), lambda bi, si: (bi, si, 0)),
            pl.BlockSpec((1, 6, d), lambda bi, si: (bi, 0, 0)),
            _full((1, d)),
            _full((n_in, d)),
            _full((MLA_Q_LORA, 1)), _full((MLA_KV_LORA, 1)),
            _full((w_qb.shape[1], MLA_Q_LORA)), _full((w_kvb.shape[1], MLA_KV_LORA)),
            _full((MLA_NOPE + MLA_ROPE, 1)), _full((MLA_NOPE + MLA_ROPE, 1)),
            _full((GQA_HEAD_DIM, 1)), _full((GQA_HEAD_DIM, 1)),
            tab(MLA_ROPE), tab(MLA_ROPE), tab(GQA_HEAD_DIM), tab(GQA_HEAD_DIM),
        ],
        out_specs=[
            pl.BlockSpec((1, MLA_HEADS, 1, 1, HEAD_PAD, tm), lambda bi, si: (bi, 0, 0, si, 0, 0)),
            pl.BlockSpec((1, MLA_HEADS, tm, HEAD_PAD), lambda bi, si: (bi, 0, si, 0)),
            pl.BlockSpec((1, MLA_HEADS, 1, dve, tm), lambda bi, si: (bi, 0, si, 0, 0)),
            pl.BlockSpec((1, GQA_Q_HEADS, 1, 1, HEAD_PAD, tm), lambda bi, si: (bi, 0, 0, si, 0, 0)),
            pl.BlockSpec((1, 1, tm, HEAD_PAD), lambda bi, si: (bi, 0, si, 0)),
            pl.BlockSpec((1, GQA_KV_HEADS, 1, dve, tm), lambda bi, si: (bi, 0, si, 0, 0)),
        ],
        compiler_params=_cparams(("parallel", "parallel")),
        name="front_ab",
    )(x, mod, g_mix.reshape(1, d), w_in.T.astype(BF16), _col(q_a_norm), _col(kv_a_norm),
      w_qb.T.astype(BF16), w_kvb.T.astype(BF16), _col(mla_qn), _col(mla_kn), _col(gqa_qn), _col(gqa_kn),
      cm, sm, cg, sg)
    return outs


def _front_c_kernel(x_ref, mod_ref, gmix_ref, wint_ref, qn_ref, kn_ref, q_ref, k_ref, v_ref):
    tm = x_ref.shape[1]
    x = x_ref[0]
    shift, scale = mod_ref[0, 0:1, :], mod_ref[0, 1:2, :]
    h = _rms_tok(x, gmix_ref[...]) * (1.0 + scale) + shift
    proj = lax.dot_general(wint_ref[...], h.astype(BF16), _NT, preferred_element_type=F32)
    nq = DIFF_HEADS * 2 * DIFF_QK
    qn, kn = qn_ref[...], kn_ref[...]
    qscale = LOG2E / math.sqrt(DIFF_QK)
    zpad = jnp.zeros((DIFF_QK, tm), F32)
    ones_blk = _ones_block(tm)
    for hd in range(DIFF_HEADS):
        kparts = []
        for c in range(2):
            r0 = (hd * 2 + c) * DIFF_QK
            qc = _rms_rows(proj[r0:r0 + DIFF_QK], qn[c * DIFF_QK:(c + 1) * DIFF_QK]) * qscale
            parts = [qc, zpad] if c == 0 else [zpad, qc]
            q_ref[0, hd, c, 0] = jnp.concatenate(parts, axis=0).astype(BF16)
            kparts.append(_rms_rows(proj[nq + r0:nq + r0 + DIFF_QK], kn[c * DIFF_QK:(c + 1) * DIFF_QK]))
        k_ref[0, hd] = jnp.concatenate(kparts, axis=0).T.astype(BF16)
        vh = proj[2 * nq + hd * DIFF_V:2 * nq + (hd + 1) * DIFF_V]
        v_ref[0, hd, 0] = jnp.concatenate([vh, ones_blk], axis=0).astype(BF16)


def _front_c(x, mod, g_mix, w_in, qn, kn):
    b, s, d = x.shape
    tm = TOK_TILE
    nt = s // tm
    dve = DIFF_V + ONES_ROWS
    return pl.pallas_call(
        _front_c_kernel,
        out_shape=[
            jax.ShapeDtypeStruct((b, DIFF_HEADS, 2, nt, HEAD_PAD, tm), BF16),
            jax.ShapeDtypeStruct((b, DIFF_HEADS, s, HEAD_PAD), BF16),
            jax.ShapeDtypeStruct((b, DIFF_HEADS, nt, dve, tm), BF16),
        ],
        grid=(b, nt),
        in_specs=[
            pl.BlockSpec((1, tm, d), lambda bi, si: (bi, si, 0)),
            pl.BlockSpec((1, 6, d), lambda bi, si: (bi, 0, 0)),
            _full((1, d)),
            _full((w_in.shape[1], d)),
            _full((2 * DIFF_QK, 1)), _full((2 * DIFF_QK, 1)),
        ],
        out_specs=[
            pl.BlockSpec((1, DIFF_HEADS, 2, 1, HEAD_PAD, tm), lambda bi, si: (bi, 0, 0, si, 0, 0)),
            pl.BlockSpec((1, DIFF_HEADS, tm, HEAD_PAD), lambda bi, si: (bi, 0, si, 0)),
            pl.BlockSpec((1, DIFF_HEADS, 1, dve, tm), lambda bi, si: (bi, 0, si, 0, 0)),
        ],
        compiler_params=_cparams(("parallel", "parallel")),
        name="front_c",
    )(x, mod, g_mix.reshape(1, d), w_in.T.astype(BF16), _col(qn), _col(kn))


def _attn_kernel(*refs, ncomp, dv, tk, nk, nq, nbias, bias_step, bias_lo, diff_scale):
    has_bias = nbias > 0
    if has_bias:
        lam_ref, q_ref, k_ref, v_ref, bias_ref, subln_ref, o_ref, m_sc, acc_sc, cmax_sc, *s_bufs = refs
    else:
        q_ref, k_ref, v_ref, o_ref, m_sc, acc_sc, cmax_sc, *s_bufs = refs
    tq = q_ref.shape[-1]
    m_sc[...] = jnp.full(m_sc.shape, NEG_BIG, F32)
    acc_sc[...] = jnp.zeros(acc_sc.shape, F32)

    per_iter = len(s_bufs)
    qper = per_iter // ncomp
    ipj = nq // qper
    n_iter = nk * ipj

    def where(i, k):
        j = jnp.minimum(i // ipj, nk - 1)
        return j, (i % ipj) * qper + k // ncomp, k % ncomp

    def scores(jqc, slot):
        j, qi, c = jqc
        k_t = k_ref[0, 0, pl.ds(pl.multiple_of(j * tk, tk), tk), :]
        s = jnp.dot(k_t, q_ref[0, 0, c, qi], preferred_element_type=F32)
        if has_bias:
            d = j * tk - qi * tq
            bidx = jnp.clip((d - bias_lo) // bias_step, 0, nbias - 1)
            s = s + bias_ref[0, bidx]
        s_bufs[slot][...] = s
        cmax_sc[slot] = jnp.max(s, axis=0, keepdims=True)

    def consume(slot, jqc):
        j, qi, c = jqc
        m_prev = m_sc[c, qi]
        m_new = jnp.maximum(m_prev, cmax_sc[slot])
        m_sc[c, qi] = m_new
        alpha = jnp.exp2(m_prev - m_new)
        p = jnp.exp2(s_bufs[slot][...] - m_new).astype(BF16)
        acc_sc[c, qi] = alpha * acc_sc[c, qi] + jnp.dot(v_ref[0, 0, j], p, preferred_element_type=F32)

    half = 2
    for k in range(half):
        scores(where(0, k), k)

    def body(i, carry):
        for k in range(per_iter):
            consume(k, where(i, k))
            nk_ = k + half
            scores(where(i, nk_) if nk_ < per_iter else where(i + 1, nk_ - per_iter), nk_ % per_iter)
        return carry

    lax.fori_loop(0, n_iter, body, 0)

    for qi in range(nq):
        cols = slice(qi * tq, (qi + 1) * tq)
        if ncomp == 1:
            acc = acc_sc[0, qi]
            o_ref[0, :, cols] = (acc[:dv] / acc[dv:dv + 1]).astype(o_ref.dtype)
        else:
            a0, a1 = acc_sc[0, qi], acc_sc[1, qi]
            o = a0[:dv] / a0[dv:dv + 1] - lam_ref[0] * (a1[:dv] / a1[dv:dv + 1])
            o = _rms_rows(o, subln_ref[...]) * diff_scale
            o_ref[0, :, cols] = o.astype(o_ref.dtype)


def _attention(q_t, k, v_t, *, dv, k_div, v_div, bias=None, bias_step=0, bias_lo=0, lam=None, subln=None,
               diff_scale=1.0):
    b, hq, ncomp, nq, dpad, tq = q_t.shape
    nk, dve, tk = v_t.shape[2:]
    s = nq * tq
    nbias = 0 if bias is None else bias.shape[1]
    in_specs = [
        pl.BlockSpec((1, 1, ncomp, nq, dpad, tq), lambda bi, hi: (bi, hi, 0, 0, 0, 0)),
        pl.BlockSpec((1, 1, s, dpad), lambda bi, hi: (bi, hi // k_div, 0, 0)),
        pl.BlockSpec((1, 1, nk, dve, tk), lambda bi, hi: (bi, hi // v_div, 0, 0, 0)),
    ]
    args = [q_t, k, v_t]
    if bias is not None:
        in_specs = [pl.BlockSpec(memory_space=pltpu.SMEM)] + in_specs + [
            pl.BlockSpec((1, nbias, tk, tq), lambda bi, hi: (hi, 0, 0, 0)),
            _full((dv, 1)),
        ]
        args = [lam] + args + [bias, subln]
    kern = functools.partial(_attn_kernel, ncomp=ncomp, dv=dv, tk=tk, nk=nk, nq=nq, nbias=nbias,
                             bias_step=bias_step, bias_lo=bias_lo, diff_scale=diff_scale)
    return pl.pallas_call(
        kern,
        out_shape=jax.ShapeDtypeStruct((b, hq * dv, s), BF16),
        grid=(b, hq),
        in_specs=in_specs,
        out_specs=pl.BlockSpec((1, dv, s), lambda bi, hi: (bi, hi, 0)),
        scratch_shapes=[pltpu.VMEM((ncomp, nq, 1, tq), F32), pltpu.VMEM((ncomp, nq, dve, tq), F32),
                        pltpu.VMEM((ATTN_ITEMS, 1, tq), F32)] + [pltpu.VMEM((tk, tq), F32)] * ATTN_ITEMS,
        compiler_params=pltpu.CompilerParams(dimension_semantics=("parallel", "parallel"),
                                             vmem_limit_bytes=ATTN_VMEM_LIMIT),
        name="attn_diff" if ncomp == 2 else "attn_softmax",
    )(*args)


def _t5_bucket(rel):
    nb = REL_BUCKETS // 2
    max_exact = nb // 2
    ret = jnp.where(rel > 0, nb, 0)
    n = jnp.abs(rel)
    nf = jnp.maximum(n, 1).astype(F32)
    large = max_exact + (jnp.log(nf / max_exact) / math.log(REL_MAX_DIST / max_exact) * (nb - max_exact)).astype(I32)
    large = jnp.minimum(large, nb - 1)
    return ret + jnp.where(n < max_exact, n, large)


def _bias_tiles(rel_bias, tk, tq):
    step = math.gcd(tk, tq)
    lo = -(-(REL_MAX_DIST - 1 + tk) // step) * step
    hi = -(-(REL_MAX_DIST - 1 + tq) // step) * step
    lo = -lo
    offs = jnp.arange(lo, hi + 1, step, dtype=I32)
    rel = offs[:, None, None] + jnp.arange(tk, dtype=I32)[None, :, None] - jnp.arange(tq, dtype=I32)[None, None, :]
    bucket = _t5_bucket(rel)[None]
    rb = rel_bias.astype(F32) * LOG2E
    tiles = jnp.zeros((rel_bias.shape[1],) + rel.shape, F32)
    for bkt in range(REL_BUCKETS):
        tiles = jnp.where(bucket == bkt, rb[bkt][:, None, None, None], tiles)
    return tiles, step, lo


def _post_kernel(*refs, n_o):
    o_refs = refs[:n_o]
    (woutt_ref, x_ref, mod_ref, gffn_ref, wrt_ref, br_ref,
     x1_ref, h2_ref, ri_ref, rf_ref, cnt_ref, carry_sc) = refs[n_o:]
    tm = x_ref.shape[1]
    first = jnp.logical_and(pl.program_id(0) == 0, pl.program_id(1) == 0)

    @pl.when(first)
    def _():
        carry_sc[...] = jnp.zeros(carry_sc.shape, F32)

    off = 0
    yt = None
    for o_ref in o_refs:
        n = o_ref.shape[1]
        part = jnp.dot(woutt_ref[:, off:off + n], o_ref[0], preferred_element_type=F32)
        yt = part if yt is None else yt + part
        off += n
    y = yt.T
    gate1 = mod_ref[0, 2:3, :]
    x1 = x_ref[0] + gate1 * y
    x1_ref[0] = x1
    shift2, scale2 = mod_ref[0, 3:4, :], mod_ref[0, 4:5, :]
    h2 = _rms_tok(x1, gffn_ref[...]) * (1.0 + scale2) + shift2
    h2_ref[0] = h2

    lg = lax.dot_general(wrt_ref[...], h2, _NT, preferred_element_type=F32,
                         precision=lax.Precision.HIGHEST) + br_ref[...]
    g_log = lg[0:MOE_GROUPS]
    gmax = jnp.max(g_log, axis=0, keepdims=True)
    iota_g = lax.broadcasted_iota(I32, (MOE_GROUPS, tm), 0)
    g_idx = jnp.min(jnp.where(g_log == gmax, iota_g, MOE_GROUPS), axis=0, keepdims=True)
    pg_top = 1.0 / jnp.sum(jnp.exp(g_log - gmax), axis=0, keepdims=True)
    el = jnp.zeros((MOE_EPG, tm), F32)
    for g in range(MOE_GROUPS):
        el = el + jnp.where(g_idx == g, lg[MOE_GROUPS + g * MOE_EPG:MOE_GROUPS + (g + 1) * MOE_EPG], 0.0)
    iota_e = lax.broadcasted_iota(I32, (MOE_EPG, tm), 0)
    e_max = jnp.max(el, axis=0, keepdims=True)
    i0 = jnp.min(jnp.where(el == e_max, iota_e, MOE_EPG), axis=0, keepdims=True)
    el2 = jnp.where(iota_e == i0, -jnp.inf, el)
    e_max2 = jnp.max(el2, axis=0, keepdims=True)
    i1 = jnp.min(jnp.where(el2 == e_max2, iota_e, MOE_EPG), axis=0, keepdims=True)
    p1 = jnp.exp(e_max2 - e_max)
    w0 = pg_top / (1.0 + p1)
    w1 = pg_top * p1 / (1.0 + p1)
    e0 = g_idx * MOE_EPG + i0
    e1 = g_idx * MOE_EPG + i1

    iota_x = lax.broadcasted_iota(I32, (MOE_EXPERTS, tm), 0)
    hit0, hit1 = iota_x == e0, iota_x == e1
    oh = jnp.logical_or(hit0, hit1).astype(BF16)
    upper = (lax.broadcasted_iota(I32, (tm, tm), 0) < lax.broadcasted_iota(I32, (tm, tm), 1)).astype(BF16)
    prefix = jnp.dot(oh, upper, preferred_element_type=F32) + carry_sc[:, 0:1]
    rank0 = jnp.sum(jnp.where(hit0, prefix, 0.0), axis=0, keepdims=True)
    rank1 = jnp.sum(jnp.where(hit1, prefix, 0.0), axis=0, keepdims=True)
    total = carry_sc[...] + jnp.sum(oh.astype(F32), axis=1, keepdims=True)
    carry_sc[...] = total
    cnt_ref[...] = total.astype(I32)

    zi = jnp.zeros((4, tm), I32)
    ri_ref[0] = jnp.concatenate([e0, e1, rank0.astype(I32), rank1.astype(I32), zi], axis=0)
    rf_ref[0] = jnp.concatenate([w0, w1, jnp.zeros((6, tm), F32)], axis=0)


def _post(o_list, w_out, x, mod, g_ffn, w_group, b_group, w_expert, b_expert):
    b, s, d = x.shape
    tm = TOK_TILE
    nt = s // tm
    n_r = MOE_GROUPS + MOE_EXPERTS
    n_rp = 40
    w_r = jnp.concatenate([w_group.T, jnp.transpose(w_expert, (0, 2, 1)).reshape(MOE_EXPERTS, d),
                           jnp.zeros((n_rp - n_r, d), F32)], axis=0).astype(F32)
    b_r = jnp.concatenate([b_group, b_expert.reshape(-1), jnp.zeros((n_rp - n_r,), F32)]).reshape(n_rp, 1)
    n_o = len(o_list)
    in_specs = [pl.BlockSpec((1, o.shape[1], tm), lambda bi, si: (bi, 0, si)) for o in o_list] + [
        _full((d, w_out.shape[0])),
        pl.BlockSpec((1, tm, d), lambda bi, si: (bi, si, 0)),
        pl.BlockSpec((1, 6, d), lambda bi, si: (bi, 0, 0)),
        _full((1, d)),
        _full((n_rp, d)), _full((n_rp, 1)),
    ]
    return pl.pallas_call(
        functools.partial(_post_kernel, n_o=n_o),
        out_shape=[
            jax.ShapeDtypeStruct((b, s, d), F32),
            jax.ShapeDtypeStruct((b, s, d), F32),
            jax.ShapeDtypeStruct((b, 8, s), I32),
            jax.ShapeDtypeStruct((b, 8, s), F32),
            jax.ShapeDtypeStruct((MOE_EXPERTS, 128), I32),
        ],
        grid=(b, nt),
        in_specs=in_specs,
        out_specs=[
            pl.BlockSpec((1, tm, d), lambda bi, si: (bi, si, 0)),
            pl.BlockSpec((1, tm, d), lambda bi, si: (bi, si, 0)),
            pl.BlockSpec((1, 8, tm), lambda bi, si: (bi, 0, si)),
            pl.BlockSpec((1, 8, tm), lambda bi, si: (bi, 0, si)),
            _full((MOE_EXPERTS, 128)),
        ],
        scratch_shapes=[pltpu.VMEM((MOE_EXPERTS, 128), F32)],
        compiler_params=_cparams(("arbitrary", "arbitrary")),
        name="post_router",
    )(*o_list, w_out.T.astype(BF16), x, mod, g_ffn.reshape(1, d), w_r, b_r)


def _row_copy(src_ref, row, dst_ref, slot, sem):
    return pltpu.make_async_copy(src_ref.at[pl.ds(row, 1)], dst_ref.at[pl.ds(slot, 1)], sem)


def _dispatch_kernel(pe_ref, pn_ref, dst_ref, h2_ref, xs_ref, zero_sc, sem, zsem):
    i = pl.program_id(0)
    td = h2_ref.shape[0]
    bm = zero_sc.shape[0]

    @pl.when(i == 0)
    def _():
        zero_sc[...] = jnp.zeros(zero_sc.shape, F32)

        def zbody(e, carry):
            @pl.when(pn_ref[e] > 0)
            def _():
                start = pl.multiple_of(pe_ref[e] - bm, bm)
                cp = pltpu.make_async_copy(zero_sc, xs_ref.at[pl.ds(start, bm)], zsem)
                cp.start()
                cp.wait()
            return carry

        lax.fori_loop(0, MOE_EXPERTS, zbody, 0)

        def tbody(blk, carry):
            cp = pltpu.make_async_copy(zero_sc, xs_ref.at[pl.ds(pl.multiple_of(blk * bm, bm), bm)], zsem)
            cp.start()
            cp.wait()
            return carry

        lax.fori_loop(pe_ref[MOE_EXPERTS - 1] // bm, xs_ref.shape[0] // bm, tbody, 0)

    for k in range(MOE_TOP_K):
        def body(g, carry, k=k):
            r0 = pl.multiple_of(g * 8, 8)
            slab = h2_ref.at[pl.ds(r0, 8)]
            for u in range(8):
                _row_copy(slab, u, xs_ref, dst_ref[0, k, r0 + u], sem).start()
            return carry
        lax.fori_loop(0, td // 8, body, 0)
    for k in range(MOE_TOP_K):
        pltpu.make_async_copy(h2_ref, xs_ref.at[pl.ds(0, td)], sem).wait()


def _dispatch(h2, dest, pad_end, padded, p_rows):
    t, d = h2.shape
    td = DISPATCH_TILE
    grid_spec = pltpu.PrefetchScalarGridSpec(
        num_scalar_prefetch=2,
        grid=(t // td,),
        in_specs=[
            pl.BlockSpec((1, MOE_TOP_K, td), lambda i, pe, pn: (i, 0, 0), memory_space=pltpu.SMEM),
            pl.BlockSpec((td, d), lambda i, pe, pn: (i, 0)),
        ],
        out_specs=pl.BlockSpec(memory_space=pl.ANY),
        scratch_shapes=[pltpu.VMEM((MOE_BLOCK, d), F32), pltpu.SemaphoreType.DMA(()),
                        pltpu.SemaphoreType.DMA(())],
    )
    return pl.pallas_call(
        _dispatch_kernel,
        out_shape=jax.ShapeDtypeStruct((p_rows, d), F32),
        grid_spec=grid_spec,
        compiler_params=_cparams(("arbitrary",)),
        name="moe_dispatch",
    )(pad_end, padded, dest, h2)


def _expert_kernel(be_ref, nu_ref, xs_ref, wup_ref, wdn_ref, ys_ref, wup_sc, wdn_sc):
    i = pl.program_id(0)
    n_used = nu_ref[0]
    used = i < n_used
    fresh = jnp.logical_or(i == 0, be_ref[i] != be_ref[jnp.maximum(i - 1, 0)])

    @pl.when(jnp.logical_and(used, fresh))
    def _():
        wup_sc[...] = wup_ref[0].astype(BF16)
        wdn_sc[...] = wdn_ref[0].astype(BF16)

    @pl.when(used)
    def _():
        f = wdn_sc.shape[0]
        gu = jnp.dot(xs_ref[...].astype(BF16), wup_sc[...], preferred_element_type=F32)
        g, u = gu[:, :f], gu[:, f:]
        a = (g / (1.0 + jnp.exp(-g))) * u
        ys_ref[...] = jnp.dot(a.astype(BF16), wdn_sc[...], preferred_element_type=F32)

    @pl.when(jnp.logical_not(used))
    def _():
        ys_ref[...] = jnp.zeros(ys_ref.shape, F32)


def _experts(xs, block_e, n_used, w_up, w_down):
    p_rows, d = xs.shape
    bm = MOE_BLOCK
    n_blocks = p_rows // bm
    f2 = w_up.shape[2]
    f = f2 // 2
    grid_spec = pltpu.PrefetchScalarGridSpec(
        num_scalar_prefetch=2,
        grid=(n_blocks,),
        in_specs=[
            pl.BlockSpec((bm, d), lambda i, be, nu: (jnp.minimum(i, nu[0] - 1), 0)),
            pl.BlockSpec((1, d, f2), lambda i, be, nu: (be[i], 0, 0)),
            pl.BlockSpec((1, f, d), lambda i, be, nu: (be[i], 0, 0)),
        ],
        out_specs=pl.BlockSpec((bm, d), lambda i, be, nu: (i, 0)),
        scratch_shapes=[pltpu.VMEM((d, f2), BF16), pltpu.VMEM((f, d), BF16)],
    )
    return pl.pallas_call(
        _expert_kernel,
        out_shape=jax.ShapeDtypeStruct((p_rows, d), F32),
        grid_spec=grid_spec,
        compiler_params=_cparams(("arbitrary",)),
        name="moe_experts",
    )(block_e, n_used, xs, w_up, w_down)


def _combine_kernel(cur_ref, nxt_ref, x_ref, mod_ref, gw_ref, ys_ref, o_ref, buf, sem):
    i = pl.program_id(0)
    tc = x_ref.shape[0]
    slot = i % 2

    def issue(idx_ref, sl):
        for k in range(MOE_TOP_K):
            def body(g, carry, k=k):
                r0 = pl.multiple_of(g * 8, 8)
                slab = buf.at[sl, k, pl.ds(r0, 8)]
                for u in range(8):
                    _row_copy(ys_ref, idx_ref[0, k, r0 + u], slab, u, sem.at[sl]).start()
                return carry
            lax.fori_loop(0, tc // 8, body, 0)

    @pl.when(i == 0)
    def _():
        issue(cur_ref, 0)

    @pl.when(i + 1 < pl.num_programs(0))
    def _():
        issue(nxt_ref, 1 - slot)

    for k in range(MOE_TOP_K):
        pltpu.make_async_copy(ys_ref.at[pl.ds(0, tc)], buf.at[slot, k], sem.at[slot]).wait()
    acc = None
    for k in range(MOE_TOP_K):
        term = buf[slot, k] * gw_ref[:, k:k + 1]
        acc = term if acc is None else acc + term
    o_ref[...] = x_ref[...] + mod_ref[0, 5:6, :] * acc


def _combine(x1, mod, dest, gates, ys):
    t, d = x1.shape
    tc = COMBINE_TILE
    b = mod.shape[0]
    per_b = t // b // tc
    n = t // tc
    return pl.pallas_call(
        _combine_kernel,
        out_shape=jax.ShapeDtypeStruct((t, d), F32),
        grid=(n,),
        in_specs=[
            pl.BlockSpec((1, MOE_TOP_K, tc), lambda i: (i, 0, 0), memory_space=pltpu.SMEM),
            pl.BlockSpec((1, MOE_TOP_K, tc), lambda i: (jnp.minimum(i + 1, n - 1), 0, 0),
                         memory_space=pltpu.SMEM),
            pl.BlockSpec((tc, d), lambda i: (i, 0)),
            pl.BlockSpec((1, 6, d), lambda i: (i // per_b, 0, 0)),
            pl.BlockSpec((tc, MOE_TOP_K), lambda i: (i, 0)),
            pl.BlockSpec(memory_space=pl.ANY),
        ],
        out_specs=pl.BlockSpec((tc, d), lambda i: (i, 0)),
        scratch_shapes=[pltpu.VMEM((2, MOE_TOP_K, tc, d), F32), pltpu.SemaphoreType.DMA((2,))],
        compiler_params=_cparams(("arbitrary",)),
        name="moe_combine",
    )(dest, dest, x1, mod, gates, ys)


def _tile_major(a, tile):
    b, k, s = a.shape
    return a.reshape(b, k, s // tile, tile).transpose(0, 2, 1, 3).reshape(b * s // tile, k, tile)


def _moe(x1, h2, ri, rf, counts, mod, w_up, w_down):
    b, s, d = x1.shape
    t = b * s
    bm = MOE_BLOCK
    n_blocks = (t * MOE_TOP_K) // bm + MOE_EXPERTS
    cnt = counts[:, 0]
    padded = (cnt + bm - 1) // bm * bm
    pad_end = jnp.cumsum(padded)
    pad_start = pad_end - padded
    n_used = (pad_end[-1] // bm).astype(I32).reshape(1)
    blk0 = jnp.arange(n_blocks, dtype=I32) * bm
    block_e = jnp.minimum(jnp.sum((pad_end[None, :] <= blk0[:, None]).astype(I32), axis=1), MOE_EXPERTS - 1)
    e = ri[:, 0:2, :]
    dest = ri[:, 2:4, :]
    for ex in range(MOE_EXPERTS):
        dest = dest + jnp.where(e == ex, pad_start[ex], 0)
    xs = _dispatch(h2.reshape(t, d), _tile_major(dest, DISPATCH_TILE), pad_end.astype(I32),
                   padded.astype(I32), n_blocks * bm)
    ys = _experts(xs, block_e, n_used, w_up, w_down)
    gates = rf[:, 0:2, :].transpose(0, 2, 1).reshape(t, MOE_TOP_K)
    out = _combine(x1.reshape(t, d), mod, _tile_major(dest, COMBINE_TILE), gates, ys)
    return out.reshape(b, s, d)


def kernel(x, c, ada_w, ada_b, norm_mix, norm_ffn, rel_bias, ab_w_in, ab_q_a_norm, ab_w_qb, ab_kv_a_norm, ab_w_kvb, ab_mla_qn, ab_mla_kn, ab_gqa_qn, ab_gqa_kn, ab_w_out, c_w_in, c_qn, c_kn, c_lam_q1, c_lam_k1, c_lam_q2, c_lam_k2, c_subln, c_w_out, moe_w_group, moe_b_group, moe_w_expert, moe_b_expert, moe_w_up, moe_w_down):
    depth = ada_w.shape[0]
    mod_all = _ada_mod(c, ada_w, ada_b)
    for l in range(depth):
        mod = mod_all[l]
        i = l // 2
        if l % 2 == 0:
            qa, ka, va, qb, kb, vb = _front_ab(x, mod, norm_mix[l], ab_w_in[i], ab_q_a_norm[i], ab_w_qb[i],
                                               ab_kv_a_norm[i], ab_w_kvb[i], ab_mla_qn[i], ab_mla_kn[i],
                                               ab_gqa_qn[i], ab_gqa_kn[i])
            o_a = _attention(qa, ka, va, dv=MLA_V, k_div=1, v_div=1)
            o_b = _attention(qb, kb, vb, dv=GQA_HEAD_DIM, k_div=GQA_Q_HEADS, v_div=GQA_Q_HEADS // GQA_KV_HEADS)
            o_list, w_out = [o_a, o_b], ab_w_out[i]
        else:
            lambda_init = 0.8 - 0.6 * math.exp(-0.3 * l)
            q, k, v = _front_c(x, mod, norm_mix[l], c_w_in[i], c_qn[i], c_kn[i])
            lam = (jnp.exp(jnp.sum(c_lam_q1[i].astype(F32) * c_lam_k1[i].astype(F32)))
                   - jnp.exp(jnp.sum(c_lam_q2[i].astype(F32) * c_lam_k2[i].astype(F32))) + lambda_init)
            tiles, step, lo = _bias_tiles(rel_bias, TOK_TILE, TOK_TILE)
            o_c = _attention(q, k, v, dv=DIFF_V, k_div=1, v_div=1, bias=tiles, bias_step=step, bias_lo=lo,
                             lam=lam.reshape(1).astype(F32), subln=_col(c_subln[i]),
                             diff_scale=1.0 - lambda_init)
            o_list, w_out = [o_c], c_w_out[i]
        x1, h2, ri, rf, counts = _post(o_list, w_out, x, mod, norm_ffn[l], moe_w_group[l], moe_b_group[l],
                                       moe_w_expert[l], moe_b_expert[l])
        x = _moe(x1, h2, ri, rf, counts, mod, moe_w_up[l], moe_w_down[l])
    return x
```
